```python
import jax, jax.numpy as jnp
from jax import lax
import numpy as np

D_MODEL = 2048
BATCH = 4
SEQ = 8192
DEPTH = 4
DEC_BATCH = 32
DEC_SEQ = 32
PAST_LEN = 1024

CHUNK = 64
N_MIXERS = 2
N_A_LAYERS = (DEPTH + 1) // 2
N_B_LAYERS = DEPTH // 2
N_DENSE_LAYERS = (DEPTH + 1) // 2
N_MOE_LAYERS = DEPTH // 2
CONV_A_WIDTH = 3
CONV_B_WIDTH = 31
D_FF = 5632
N_EXPERTS = 8
TOP_K = 2
D_FF_EXPERT = 7168
RMS_EPS = 1e-6
LN_EPS = 1e-5

kernel_name = "hybrid_streaming_conv_encoder_step"


def rms_norm(x, g):
    x32 = x.astype(jnp.float32)
    y = x32 * lax.rsqrt(jnp.mean(x32 * x32, axis=-1, keepdims=True) + RMS_EPS)
    return y.astype(x.dtype) * g


def layer_norm(x, g, b):
    x32 = x.astype(jnp.float32)
    mu = jnp.mean(x32, axis=-1, keepdims=True)
    xc = x32 - mu
    var = jnp.mean(xc * xc, axis=-1, keepdims=True)
    return (xc * lax.rsqrt(var + LN_EPS)).astype(x.dtype) * g + b


def causal_dwconv(u, ctx, w):
    k = w.shape[0]
    full = jnp.concatenate([ctx.astype(u.dtype), u], axis=1)
    y = lax.conv_general_dilated(
        full, w[:, None, :].astype(u.dtype), window_strides=(1,), padding="VALID",
        dimension_numbers=("NWC", "WIO", "NWC"), feature_group_count=u.shape[-1])
    return y, full[:, -(k - 1):]


def short_conv_mixer(h, ctx, w_in, w_conv, w_out):
    b_gate, c_gate, v = jnp.split(h @ w_in, 3, axis=-1)
    conv, new_ctx = causal_dwconv(c_gate * v, ctx, w_conv)
    return (b_gate * conv) @ w_out, new_ctx


def conformer_conv_mixer(h, ctx, w_pw1, w_dw, b_dw, ln_g, ln_b, w_pw2):
    a, g = jnp.split(h @ w_pw1, 2, axis=-1)
    u = a * jax.nn.sigmoid(g)
    conv, new_ctx = causal_dwconv(u, ctx, w_dw)
    z = jax.nn.silu(layer_norm(conv + b_dw, ln_g, ln_b))
    return z @ w_pw2, new_ctx


def swiglu(h, w_gate, w_up, w_down):
    return (jax.nn.silu(h @ w_gate) * (h @ w_up)) @ w_down


def moe_swiglu(h, w_router, w_gate, w_up, w_down):
    logits = (h @ w_router).astype(jnp.float32)
    top_val, top_idx = lax.top_k(logits, TOP_K)
    top_w = jax.nn.softmax(top_val, axis=-1)
    combine = jnp.sum(jax.nn.one_hot(top_idx, N_EXPERTS, dtype=jnp.float32) * top_w[..., None],
                      axis=-2).astype(h.dtype)
    out = jnp.zeros_like(h)
    for e in range(N_EXPERTS):
        out = out + combine[..., e:e + 1] * swiglu(h, w_gate[e], w_up[e], w_down[e])
    return out


def trunk(x, ctx_a, ctx_b, mixer_norm, ffn_norm, final_norm,
          a_w_in, a_w_conv, a_w_out,
          b_w_pw1, b_w_dw, b_dw_bias, b_ln_g, b_ln_b, b_w_pw2,
          ffn_w_gate, ffn_w_up, ffn_w_down,
          moe_w_router, moe_w_gate, moe_w_up, moe_w_down):
    new_a, new_b = [], []
    for i in range(DEPTH):
        j = i // N_MIXERS
        h = rms_norm(x, mixer_norm[i])
        if i % N_MIXERS == 0:
            m, c = short_conv_mixer(h, ctx_a[j], a_w_in[j], a_w_conv[j], a_w_out[j])
            new_a.append(c)
        else:
            m, c = conformer_conv_mixer(h, ctx_b[j], b_w_pw1[j], b_w_dw[j], b_dw_bias[j],
                                        b_ln_g[j], b_ln_b[j], b_w_pw2[j])
            new_b.append(c)
        x = x + m
        h = rms_norm(x, ffn_norm[i])
        k = i // 2
        if i % 2 == 0:
            f = swiglu(h, ffn_w_gate[k], ffn_w_up[k], ffn_w_down[k])
        else:
            f = moe_swiglu(h, moe_w_router[k], moe_w_gate[k], moe_w_up[k], moe_w_down[k])
        x = x + f
    return rms_norm(x, final_norm), jnp.stack(new_a), jnp.stack(new_b)


def setup_inputs(seed: int = 0) -> dict:
    key = jax.random.key(seed)
    ks = jax.random.split(key, 24)

    def nrm(k, shape, scale):
        return jax.random.normal(k, shape, dtype=jnp.float32) * scale

    D = D_MODEL
    return {
        "x_prompt": nrm(ks[0], (BATCH, SEQ, D), 1.0),
        "x_sample": nrm(ks[1], (DEC_BATCH, DEC_SEQ, D), 1.0),
        "cache_conv_a": nrm(ks[2], (N_A_LAYERS, DEC_BATCH, CONV_A_WIDTH - 1, D), 1.0),
        "cache_conv_b": nrm(ks[3], (N_B_LAYERS, DEC_BATCH, CONV_B_WIDTH - 1, D), 0.5),
        "mixer_norm": 1.0 + nrm(ks[4], (DEPTH, D), 0.02),
        "ffn_norm": 1.0 + nrm(ks[5], (DEPTH, D), 0.02),
        "final_norm": 1.0 + nrm(ks[6], (D,), 0.02),
        "a_w_in": nrm(ks[7], (N_A_LAYERS, D, 3 * D), D ** -0.5),
        "a_w_conv": nrm(ks[8], (N_A_LAYERS, CONV_A_WIDTH, D), CONV_A_WIDTH ** -0.5),
        "a_w_out": nrm(ks[9], (N_A_LAYERS, D, D), D ** -0.5),
        "b_w_pw1": nrm(ks[10], (N_B_LAYERS, D, 2 * D), D ** -0.5),
        "b_w_dw": nrm(ks[11], (N_B_LAYERS, CONV_B_WIDTH, D), CONV_B_WIDTH ** -0.5),
        "b_dw_bias": nrm(ks[12], (N_B_LAYERS, D), 0.02),
        "b_ln_g": 1.0 + nrm(ks[13], (N_B_LAYERS, D), 0.02),
        "b_ln_b": nrm(ks[14], (N_B_LAYERS, D), 0.02),
        "b_w_pw2": nrm(ks[15], (N_B_LAYERS, D, D), D ** -0.5),
        "ffn_w_gate": nrm(ks[16], (N_DENSE_LAYERS, D, D_FF), D ** -0.5),
        "ffn_w_up": nrm(ks[17], (N_DENSE_LAYERS, D, D_FF), D ** -0.5),
        "ffn_w_down": nrm(ks[18], (N_DENSE_LAYERS, D_FF, D), D_FF ** -0.5),
        "moe_w_router": nrm(ks[19], (N_MOE_LAYERS, D, N_EXPERTS), D ** -0.5),
        "moe_w_gate": nrm(ks[20], (N_MOE_LAYERS, N_EXPERTS, D, D_FF_EXPERT), D ** -0.5),
        "moe_w_up": nrm(ks[21], (N_MOE_LAYERS, N_EXPERTS, D, D_FF_EXPERT), D ** -0.5),
        "moe_w_down": nrm(ks[22], (N_MOE_LAYERS, N_EXPERTS, D_FF_EXPERT, D), D_FF_EXPERT ** -0.5),
    }


def reference(x_prompt, x_sample, cache_conv_a, cache_conv_b, mixer_norm, ffn_norm, final_norm,
              a_w_in, a_w_conv, a_w_out,
              b_w_pw1, b_w_dw, b_dw_bias, b_ln_g, b_ln_b, b_w_pw2,
              ffn_w_gate, ffn_w_up, ffn_w_down,
              moe_w_router, moe_w_gate, moe_w_up, moe_w_down):
    weights = (mixer_norm, ffn_norm, final_norm,
               a_w_in, a_w_conv, a_w_out,
               b_w_pw1, b_w_dw, b_dw_bias, b_ln_g, b_ln_b, b_w_pw2,
               ffn_w_gate, ffn_w_up, ffn_w_down,
               moe_w_router, moe_w_gate, moe_w_up, moe_w_down)
    bp = x_prompt.shape[0]
    zero_a = jnp.zeros((N_A_LAYERS, bp, CONV_A_WIDTH - 1, x_prompt.shape[-1]), x_prompt.dtype)
    zero_b = jnp.zeros((N_B_LAYERS, bp, CONV_B_WIDTH - 1, x_prompt.shape[-1]), x_prompt.dtype)
    y_prompt, state_conv_a_prompt, state_conv_b_prompt = trunk(x_prompt, zero_a, zero_b, *weights)
    y_sample, state_conv_a_sample, state_conv_b_sample = trunk(x_sample, cache_conv_a, cache_conv_b, *weights)
    return (y_prompt, y_sample, state_conv_a_prompt, state_conv_b_prompt, state_conv_a_sample, state_conv_b_sample)
```

```python
import functools
from typing import NamedTuple

import jax
import jax.numpy as jnp
from jax import lax
from jax.experimental import pallas as pl
from jax.experimental.pallas import tpu as pltpu

RMS_EPS = 1e-6
LN_EPS = 1e-5

V7X_VMEM_LIMIT_BYTES = 56 * 1024 * 1024
LANES = 128
BF16_SUBLANES = 16
I32_1D_TILE = 1024

F32 = jnp.float32
BF16 = jnp.bfloat16
I32 = jnp.int32


class Tiling(NamedTuple):
    tm: int
    n_prompt_tiles: int
    tiles_per_stream: int
    n_sample_tiles: int
    s_streams: int
    s_len: int
    tn: int
    tf: int
    te: int
    tc: int


def _params(n_grid):
    return pltpu.CompilerParams(
        dimension_semantics=("arbitrary",) * n_grid,
        vmem_limit_bytes=V7X_VMEM_LIMIT_BYTES)


def _dot(a, b):
    return jnp.dot(a, b, preferred_element_type=F32)


def _rms(x, g):
    ms = jnp.mean(x * x, axis=-1, keepdims=True)
    return (x * lax.rsqrt(ms + RMS_EPS)) * g


def _sigmoid(x):
    return 1.0 / (1.0 + jnp.exp(-x))


def _conv_tile(src_ref, halo_p_ref, halo_s_ref, w_ref, full_p, full_s, emit, *, t: Tiling, k_width, hp):
    i = pl.program_id(0)
    d = src_ref.shape[1]
    h = k_width - 1

    def run(full_ref, halo_fn, n_s, n_t):
        def chunk(c, carry):
            cols = pl.ds(pl.multiple_of(c * LANES, LANES), LANES)
            full_ref[:, :hp, :] = halo_fn(cols)
            full_ref[:, hp:, :] = src_ref[:, cols].astype(F32).reshape(n_s, n_t, LANES)
            acc = None
            for k in range(k_width):
                lo = hp - h + k
                term = full_ref[:, lo:lo + n_t, :] * w_ref[k:k + 1, cols].reshape(1, 1, LANES)
                acc = term if acc is None else acc + term
            emit(cols, acc.reshape(n_s * n_t, LANES))
            return carry

        lax.fori_loop(0, d // LANES, chunk, 0)

    @pl.when(i < t.n_prompt_tiles)
    def _():
        starts_stream = (i % t.tiles_per_stream) == 0

        def halo(cols):
            v = halo_p_ref[:, cols].astype(F32)
            return jnp.where(starts_stream, jnp.zeros_like(v), v)[None]

        run(full_p, halo, 1, t.tm)

    @pl.when(i >= t.n_prompt_tiles)
    def _():
        run(full_s, lambda cols: halo_s_ref[:, :, cols].astype(F32), t.s_streams, t.s_len)


def _halo_specs(t: Tiling, hp, d):
    blocks_per_tile = t.tm // hp
    prompt = pl.BlockSpec((hp, d), lambda i, j: (jnp.maximum(i * blocks_per_tile - 1, 0), 0))
    sample = pl.BlockSpec((t.s_streams, hp, d),
                          lambda i, j: (jnp.clip(i - t.n_prompt_tiles, 0, t.n_sample_tiles - 1), 0, 0))
    return prompt, sample


def _conv_scratch(t: Tiling, hp):
    return [pltpu.VMEM((1, hp + t.tm, LANES), F32), pltpu.VMEM((t.s_streams, hp + t.s_len, LANES), F32)]


def _pad_cache(cache, hp):
    return jnp.pad(cache.astype(BF16), ((0, 0), (hp - cache.shape[1], 0), (0, 0)))


def _a_in_kernel(x_ref, g_ref, wb_ref, wc_ref, wv_ref, b_out, cv_out, h_scr):
    @pl.when(pl.program_id(1) == 0)
    def _():
        h_scr[...] = _rms(x_ref[...], g_ref[...]).astype(BF16)

    h = h_scr[...]
    b_out[...] = _dot(h, wb_ref[...])
    cv_out[...] = (_dot(h, wc_ref[...]) * _dot(h, wv_ref[...])).astype(BF16)


def _a_in(x, g, w_in, t: Tiling):
    m, d = x.shape
    nj = d // t.tn
    return pl.pallas_call(
        _a_in_kernel,
        grid=(m // t.tm, nj),
        in_specs=[
            pl.BlockSpec((t.tm, d), lambda i, j: (i, 0)),
            pl.BlockSpec((1, d), lambda i, j: (0, 0)),
            pl.BlockSpec((d, t.tn), lambda i, j: (0, j)),
            pl.BlockSpec((d, t.tn), lambda i, j: (0, nj + j)),
            pl.BlockSpec((d, t.tn), lambda i, j: (0, 2 * nj + j)),
        ],
        out_specs=[pl.BlockSpec((t.tm, t.tn), lambda i, j: (i, j))] * 2,
        out_shape=[jax.ShapeDtypeStruct((m, d), F32), jax.ShapeDtypeStruct((m, d), BF16)],
        scratch_shapes=[pltpu.VMEM((t.tm, d), BF16)],
        compiler_params=_params(2),
        name="mixer_a_in",
    )(x, g.reshape(1, d), w_in, w_in, w_in)


def _a_out_kernel(b_ref, cv_ref, halo_p_ref, halo_s_ref, wconv_ref, wout_ref, x_ref, o_ref,
                  gated_scr, full_p, full_s, *, t: Tiling, k_width, hp):
    @pl.when(pl.program_id(1) == 0)
    def _():
        def emit(cols, conv):
            gated_scr[:, cols] = (b_ref[:, cols] * conv).astype(BF16)

        _conv_tile(cv_ref, halo_p_ref, halo_s_ref, wconv_ref, full_p, full_s, emit, t=t, k_width=k_width, hp=hp)

    o_ref[...] = x_ref[...] + _dot(gated_scr[...], wout_ref[...])


def _a_out(b, cv, cache, w_conv, w_out, x, t: Tiling):
    m, d = x.shape
    k_width = w_conv.shape[0]
    hp = BF16_SUBLANES
    halo_p, halo_s = _halo_specs(t, hp, d)
    return pl.pallas_call(
        functools.partial(_a_out_kernel, t=t, k_width=k_width, hp=hp),
        grid=(m // t.tm, d // t.tn),
        in_specs=[
            pl.BlockSpec((t.tm, d), lambda i, j: (i, 0)),
            pl.BlockSpec((t.tm, d), lambda i, j: (i, 0)),
            halo_p, halo_s,
            pl.BlockSpec((k_width, d), lambda i, j: (0, 0)),
            pl.BlockSpec((d, t.tn), lambda i, j: (0, j)),
            pl.BlockSpec((t.tm, t.tn), lambda i, j: (i, j)),
        ],
        out_specs=pl.BlockSpec((t.tm, t.tn), lambda i, j: (i, j)),
        out_shape=jax.ShapeDtypeStruct((m, d), F32),
        scratch_shapes=[pltpu.VMEM((t.tm, d), BF16)] + _conv_scratch(t, hp),
        compiler_params=_params(2),
        name="mixer_a_out",
    )(b, cv, cv, _pad_cache(cache, hp), w_conv, w_out, x)


def _b_in_kernel(x_ref, g_ref, wa_ref, wg_ref, u_out, h_scr):
    @pl.when(pl.program_id(1) == 0)
    def _():
        h_scr[...] = _rms(x_ref[...], g_ref[...]).astype(BF16)

    h = h_scr[...]
    u_out[...] = (_dot(h, wa_ref[...]) * _sigmoid(_dot(h, wg_ref[...]))).astype(BF16)


def _b_in(x, g, w_pw1, t: Tiling):
    m, d = x.shape
    nj = d // t.tn
    return pl.pallas_call(
        _b_in_kernel,
        grid=(m // t.tm, nj),
        in_specs=[
            pl.BlockSpec((t.tm, d), lambda i, j: (i, 0)),
            pl.BlockSpec((1, d), lambda i, j: (0, 0)),
            pl.BlockSpec((d, t.tn), lambda i, j: (0, j)),
            pl.BlockSpec((d, t.tn), lambda i, j: (0, nj + j)),
        ],
        out_specs=pl.BlockSpec((t.tm, t.tn), lambda i, j: (i, j)),
        out_shape=jax.ShapeDtypeStruct((m, d), BF16),
        scratch_shapes=[pltpu.VMEM((t.tm, d), BF16)],
        compiler_params=_params(2),
        name="mixer_b_in",
    )(x, g.reshape(1, d), w_pw1, w_pw1)


def _b_out_kernel(u_ref, halo_p_ref, halo_s_ref, wdw_ref, bdw_ref, lng_ref, lnb_ref, wpw2_ref, x_ref, o_ref,
                  z_scr, conv_scr, full_p, full_s, *, t: Tiling, k_width, hp):
    @pl.when(pl.program_id(1) == 0)
    def _():
        def emit(cols, conv):
            conv_scr[:, cols] = conv + bdw_ref[:, cols]

        _conv_tile(u_ref, halo_p_ref, halo_s_ref, wdw_ref, full_p, full_s, emit, t=t, k_width=k_width, hp=hp)
        c = conv_scr[...]
        mu = jnp.mean(c, axis=-1, keepdims=True)
        xc = c - mu
        var = jnp.mean(xc * xc, axis=-1, keepdims=True)
        y = (xc * lax.rsqrt(var + LN_EPS)) * lng_ref[...] + lnb_ref[...]
        z_scr[...] = (y * _sigmoid(y)).astype(BF16)

    o_ref[...] = x_ref[...] + _dot(z_scr[...], wpw2_ref[...])


def _b_out(u, cache, w_dw, b_dw, ln_g, ln_b, w_pw2, x, t: Tiling):
    m, d = x.shape
    k_width = w_dw.shape[0]
    hp = -(-(k_width - 1) // BF16_SUBLANES) * BF16_SUBLANES
    halo_p, halo_s = _halo_specs(t, hp, d)
    row = pl.BlockSpec((1, d), lambda i, j: (0, 0))
    return pl.pallas_call(
        functools.partial(_b_out_kernel, t=t, k_width=k_width, hp=hp),
        grid=(m // t.tm, d // t.tn),
        in_specs=[
            pl.BlockSpec((t.tm, d), lambda i, j: (i, 0)),
            halo_p, halo_s,
            pl.BlockSpec((k_width, d), lambda i, j: (0, 0)),
            row, row, row,
            pl.BlockSpec((d, t.tn), lambda i, j: (0, j)),
            pl.BlockSpec((t.tm, t.tn), lambda i, j: (i, j)),
        ],
        out_specs=pl.BlockSpec((t.tm, t.tn), lambda i, j: (i, j)),
        out_shape=jax.ShapeDtypeStruct((m, d), F32),
        scratch_shapes=[pltpu.VMEM((t.tm, d), BF16), pltpu.VMEM((t.tm, d), F32)] + _conv_scratch(t, hp),
        compiler_params=_params(2),
        name="mixer_b_out",
    )(u, u, _pad_cache(cache, hp), w_dw, b_dw.reshape(1, d), ln_g.reshape(1, d), ln_b.reshape(1, d), w_pw2, x)


def _swiglu_step(h, wg_ref, wu_ref, wd_ref):
    gate = _dot(h, wg_ref[...])
    act = (gate * _sigmoid(gate)) * _dot(h, wu_ref[...])
    return _dot(act.astype(BF16), wd_ref[...])


def _ffn_kernel(x_ref, g_ref, wg_ref, wu_ref, wd_ref, o_ref, h_scr):
    @pl.when(pl.program_id(1) == 0)
    def _():
        x = x_ref[...]
        h_scr[...] = _rms(x, g_ref[...]).astype(BF16)
        o_ref[...] = x

    o_ref[...] += _swiglu_step(h_scr[...], wg_ref, wu_ref, wd_ref)


def _ffn(x, g, w_gate, w_up, w_down, t: Tiling):
    m, d = x.shape
    f = w_gate.shape[1]
    return pl.pallas_call(
        _ffn_kernel,
        grid=(m // t.tm, f // t.tf),
        in_specs=[
            pl.BlockSpec((t.tm, d), lambda i, s: (i, 0)),
            pl.BlockSpec((1, d), lambda i, s: (0, 0)),
            pl.BlockSpec((d, t.tf), lambda i, s: (0, s)),
            pl.BlockSpec((d, t.tf), lambda i, s: (0, s)),
            pl.BlockSpec((t.tf, d), lambda i, s: (s, 0)),
        ],
        out_specs=pl.BlockSpec((t.tm, d), lambda i, s: (i, 0)),
        out_shape=jax.ShapeDtypeStruct((m, d), F32),
        scratch_shapes=[pltpu.VMEM((t.tm, d), BF16)],
        compiler_params=_params(2),
        name="dense_ffn",
    )(x, g.reshape(1, d), w_gate, w_up, w_down)


def _router_kernel(x_ref, g_ref, wr_ref, meta_ref, wts_ref, cnt_ref, run_scr, *, n_experts):
    @pl.when(pl.program_id(0) == 0)
    def _():
        run_scr[...] = jnp.zeros_like(run_scr)

    h = _rms(x_ref[...], g_ref[...]).astype(BF16)
    logits = _dot(h, wr_ref[...])
    tm = logits.shape[0]
    lane = lax.broadcasted_iota(I32, logits.shape, 1)
    neg = jnp.full_like(logits, -jnp.inf)
    l1 = jnp.where(lane < n_experts, logits, neg)
    m1 = jnp.max(l1, axis=1, keepdims=True)
    i1 = jnp.min(jnp.where(l1 == m1, lane, LANES), axis=1, keepdims=True)
    l2 = jnp.where(lane == i1, neg, l1)
    m2 = jnp.max(l2, axis=1, keepdims=True)
    i2 = jnp.min(jnp.where(l2 == m2, lane, LANES), axis=1, keepdims=True)
    e2 = jnp.exp(m2 - m1)
    w1 = 1.0 / (1.0 + e2)
    w2 = e2 / (1.0 + e2)

    sel = jnp.where((lane == i1) | (lane == i2), 1.0, 0.0)
    row = lax.broadcasted_iota(I32, (tm, tm), 0)
    col = lax.broadcasted_iota(I32, (tm, tm), 1)
    earlier = jnp.where(col < row, 1.0, 0.0).astype(BF16)
    before = _dot(earlier, sel.astype(BF16)) + run_scr[...]
    rank1 = jnp.sum(jnp.where(lane == i1, before, 0.0), axis=1, keepdims=True).astype(I32)
    rank2 = jnp.sum(jnp.where(lane == i2, before, 0.0), axis=1, keepdims=True).astype(I32)

    zero = jnp.zeros_like(lane)
    meta_ref[...] = jnp.where(lane == 0, i1, jnp.where(lane == 1, i2,
                              jnp.where(lane == 2, rank1, jnp.where(lane == 3, rank2, zero))))
    wts_ref[...] = jnp.where(lane == 0, w1, jnp.where(lane == 1, w2, 0.0))
    run_scr[...] += jnp.sum(sel, axis=0, keepdims=True)
    cnt_ref[...] = run_scr[...].astype(I32)


def _router(x, g, w_router, t: Tiling):
    m, d = x.shape
    n_experts = w_router.shape[1]
    wr = jnp.pad(w_router.astype(BF16), ((0, 0), (0, LANES - n_experts)))
    return pl.pallas_call(
        functools.partial(_router_kernel, n_experts=n_experts),
        grid=(m // t.tm,),
        in_specs=[
            pl.BlockSpec((t.tm, d), lambda i: (i, 0)),
            pl.BlockSpec((1, d), lambda i: (0, 0)),
            pl.BlockSpec((d, LANES), lambda i: (0, 0)),
        ],
        out_specs=[pl.BlockSpec((t.tm, LANES), lambda i: (i, 0)), pl.BlockSpec((t.tm, LANES), lambda i: (i, 0)),
                   pl.BlockSpec((1, LANES), lambda i: (0, 0))],
        out_shape=[jax.ShapeDtypeStruct((m, LANES), I32), jax.ShapeDtypeStruct((m, LANES), F32),
                   jax.ShapeDtypeStruct((1, LANES), I32)],
        scratch_shapes=[pltpu.VMEM((1, LANES), F32)],
        compiler_params=_params(1),
        name="moe_router",
    )(x, g.reshape(1, d), wr)


def _row_copy(src_hbm, src_row, dst_ref, dst_row, sem):
    return pltpu.make_async_copy(src_hbm.at[pl.ds(src_row, 1), :], dst_ref.at[pl.ds(dst_row, 1), :], sem)


def _experts_kernel(te_ref, nu_ref, x_hbm, src_hbm, g_ref, wg_ref, wu_ref, wd_ref, o_ref,
                    rows, h_scr, idx_smem, row_sem, idx_sem, *, nf):
    g = pl.program_id(0)
    s = pl.program_id(1)
    n_used = nu_ref[0]
    te = rows.shape[0]
    per_step = -(-te // (nf - 1))

    def idx_copy(tile):
        return pltpu.make_async_copy(src_hbm.at[pl.ds(pl.multiple_of(tile * te, te), te)], idx_smem, idx_sem)

    def issue_rows(lo, hi):
        def body(r, carry):
            _row_copy(x_hbm, idx_smem[r], rows, r, row_sem).start()
            return carry
        lax.fori_loop(lo, hi, body, 0)

    @pl.when(g < n_used)
    def _():
        @pl.when((g == 0) & (s == 0))
        def _():
            idx_copy(0).start()
            idx_copy(0).wait()
            issue_rows(0, te)

        @pl.when(s == 0)
        def _():
            pltpu.make_async_copy(x_hbm.at[pl.ds(0, te), :], rows, row_sem).wait()
            h_scr[...] = _rms(rows[...], g_ref[...]).astype(BF16)
            o_ref[...] = jnp.zeros_like(o_ref)

            @pl.when(g + 1 < n_used)
            def _():
                idx_copy(g + 1).start()

        @pl.when((s >= 1) & (g + 1 < n_used))
        def _():
            @pl.when(s == 1)
            def _():
                idx_copy(g + 1).wait()

            lo = (s - 1) * per_step
            issue_rows(lo, jnp.minimum(lo + per_step, te))

        o_ref[...] += _swiglu_step(h_scr[...], wg_ref, wu_ref, wd_ref)

    @pl.when((g >= n_used) & (s == 0))
    def _():
        o_ref[...] = jnp.zeros_like(o_ref)


def _experts(x, g, src_tok, tile_expert, n_used, w_gate, w_up, w_down, t: Tiling):
    m, d = x.shape
    n_tiles = tile_expert.shape[0]
    f = w_gate.shape[2]
    nf = f // t.tf
    assert nf >= 2 and t.te % I32_1D_TILE == 0

    def w_in_map(gi, s, te_ref, nu_ref):
        return te_ref[gi], 0, jnp.where(gi < nu_ref[0], s, nf - 1)

    def w_out_map(gi, s, te_ref, nu_ref):
        return te_ref[gi], jnp.where(gi < nu_ref[0], s, nf - 1), 0

    return pl.pallas_call(
        functools.partial(_experts_kernel, nf=nf),
        grid_spec=pltpu.PrefetchScalarGridSpec(
            num_scalar_prefetch=2,
            grid=(n_tiles, nf),
            in_specs=[
                pl.BlockSpec(memory_space=pl.ANY),
                pl.BlockSpec(memory_space=pl.ANY),
                pl.BlockSpec((1, d), lambda gi, s, te_ref, nu_ref: (0, 0)),
                pl.BlockSpec((None, d, t.tf), w_in_map),
                pl.BlockSpec((None, d, t.tf), w_in_map),
                pl.BlockSpec((None, t.tf, d), w_out_map),
            ],
            out_specs=pl.BlockSpec((t.te, d), lambda gi, s, te_ref, nu_ref: (gi, 0)),
            scratch_shapes=[
                pltpu.VMEM((t.te, d), F32),
                pltpu.VMEM((t.te, d), BF16),
                pltpu.SMEM((t.te,), I32),
                pltpu.SemaphoreType.DMA(()),
                pltpu.SemaphoreType.DMA(()),
            ]),
        out_shape=jax.ShapeDtypeStruct((n_tiles * t.te, d), F32),
        compiler_params=_params(2),
        name="moe_experts",
    )(tile_expert, n_used, x, src_tok, g.reshape(1, d), w_gate, w_up, w_down)


def _combine_kernel(x_ref, wts_ref, y_hbm, pos_hbm, o_ref, ya, yb, idx_smem, row_sem, idx_sem, *, n_tiles):
    i = pl.program_id(0)
    tc = x_ref.shape[0]
    slot = i % 2

    def idx_copy(tile, sl):
        return pltpu.make_async_copy(pos_hbm.at[pl.ds(pl.multiple_of(tile * (2 * tc), 2 * tc), 2 * tc)],
                                     idx_smem.at[sl], idx_sem.at[sl])

    def issue_rows(sl):
        def body(r, carry):
            _row_copy(y_hbm, idx_smem[sl, r], ya.at[sl], r, row_sem.at[sl]).start()
            _row_copy(y_hbm, idx_smem[sl, tc + r], yb.at[sl], r, row_sem.at[sl]).start()
            return carry
        lax.fori_loop(0, tc, body, 0)

    @pl.when(i == 0)
    def _():
        idx_copy(0, 0).start()
        idx_copy(0, 0).wait()
        issue_rows(0)

    @pl.when(i + 1 < n_tiles)
    def _():
        idx_copy(i + 1, 1 - slot).start()
        idx_copy(i + 1, 1 - slot).wait()
        issue_rows(1 - slot)

    pltpu.make_async_copy(y_hbm.at[pl.ds(0, tc), :], ya.at[slot], row_sem.at[slot]).wait()
    pltpu.make_async_copy(y_hbm.at[pl.ds(0, tc), :], yb.at[slot], row_sem.at[slot]).wait()
    w = wts_ref[...]
    o_ref[...] = x_ref[...] + w[:, 0:1] * ya[slot] + w[:, 1:2] * yb[slot]


def _combine(x, wts, y, pos_tiles, t: Tiling):
    m, d = x.shape
    n_tiles = m // t.tc
    assert (2 * t.tc) % I32_1D_TILE == 0
    return pl.pallas_call(
        functools.partial(_combine_kernel, n_tiles=n_tiles),
        grid=(n_tiles,),
        in_specs=[
            pl.BlockSpec((t.tc, d), lambda i: (i, 0)),
            pl.BlockSpec((t.tc, LANES), lambda i: (i, 0)),
            pl.BlockSpec(memory_space=pl.ANY),
            pl.BlockSpec(memory_space=pl.ANY),
        ],
        out_specs=pl.BlockSpec((t.tc, d), lambda i: (i, 0)),
        out_shape=jax.ShapeDtypeStruct((m, d), F32),
        scratch_shapes=[
            pltpu.VMEM((2, t.tc, d), F32),
            pltpu.VMEM((2, t.tc, d), F32),
            pltpu.SMEM((2, 2 * t.tc), I32),
            pltpu.SemaphoreType.DMA((2,)),
            pltpu.SemaphoreType.DMA((2,)),
        ],
        compiler_params=_params(1),
        name="moe_combine",
    )(x, wts, y, pos_tiles)


def _moe(x, g, w_router, w_gate, w_up, w_down, t: Tiling):
    m, d = x.shape
    n_experts = w_router.shape[1]
    meta, wts, counts = _router(x, g, w_router, t)

    experts, ranks = meta[:, 0:2], meta[:, 2:4]
    counts = counts[0, :n_experts]
    padded = (counts + t.te - 1) // t.te * t.te
    ends = jnp.cumsum(padded)
    pos = (ends - padded)[experts] + ranks
    n_tiles = 2 * m // t.te + n_experts
    n_used = (ends[-1] // t.te).astype(I32)
    tile_ids = jnp.minimum(jnp.arange(n_tiles, dtype=I32), n_used - 1)
    tile_expert = jnp.minimum(jnp.searchsorted(ends, tile_ids * t.te, side="right"), n_experts - 1).astype(I32)
    tokens = jnp.broadcast_to(jnp.arange(m, dtype=I32)[:, None], (m, 2))
    src_tok = jnp.zeros((n_tiles * t.te,), I32).at[pos.reshape(-1)].set(tokens.reshape(-1), unique_indices=True)
    pos_tiles = pos.reshape(m // t.tc, t.tc, 2).transpose(0, 2, 1).reshape(-1)

    y = _experts(x, g, src_tok, tile_expert, n_used.reshape(1), w_gate, w_up, w_down, t)
    return _combine(x, wts, y, pos_tiles, t)


def _final_norm_kernel(x_ref, g_ref, o_ref):
    o_ref[...] = _rms(x_ref[...], g_ref[...])


def _final_norm(x, g, t: Tiling):
    m, d = x.shape
    return pl.pallas_call(
        _final_norm_kernel,
        grid=(m // t.tm,),
        in_specs=[pl.BlockSpec((t.tm, d), lambda i: (i, 0)), pl.BlockSpec((1, d), lambda i: (0, 0))],
        out_specs=pl.BlockSpec((t.tm, d), lambda i: (i, 0)),
        out_shape=jax.ShapeDtypeStruct((m, d), F32),
        compiler_params=_params(1),
        name="final_norm",
    )(x, g.reshape(1, d))


def _tiling(bp, sp, bs, ss, tm, tn, tf, te, tc):
    assert sp % tm == 0 and tm % ss == 0 and (bs * ss) % tm == 0, (sp, ss, bs, tm)
    assert (bp * sp + bs * ss) % tc == 0
    return Tiling(tm=tm, n_prompt_tiles=bp * sp // tm, tiles_per_stream=sp // tm,
                  n_sample_tiles=bs * ss // tm, s_streams=tm // ss, s_len=ss, tn=tn, tf=tf, te=te, tc=tc)


def _last_rows(v, bp, sp, bs, ss, n):
    d = v.shape[1]
    p = v[:bp * sp].reshape(bp, sp, d)[:, sp - n:].astype(F32)
    s = v[bp * sp:].reshape(bs, ss, d)[:, ss - n:].astype(F32)
    return p, s


def _forward(x_prompt, x_sample, cache_conv_a, cache_conv_b, mixer_norm, ffn_norm, final_norm,
             a_w_in, a_w_conv, a_w_out, b_w_pw1, b_w_dw, b_dw_bias, b_ln_g, b_ln_b, b_w_pw2,
             ffn_w_gate, ffn_w_up, ffn_w_down, moe_w_router, moe_w_gate, moe_w_up, moe_w_down,
             *, tm, tn, tf, te, tc):
    bp, sp, d = x_prompt.shape
    bs, ss, _ = x_sample.shape
    depth = mixer_norm.shape[0]
    t = _tiling(bp, sp, bs, ss, tm, tn, tf, te, tc)
    ka, kb = a_w_conv.shape[1], b_w_dw.shape[1]

    x = jnp.concatenate([x_prompt.reshape(bp * sp, d), x_sample.reshape(bs * ss, d)], axis=0)
    bf = lambda w: w.astype(BF16)
    taps = lambda w: w.astype(BF16).astype(F32)
    state_a_p, state_a_s, state_b_p, state_b_s = [], [], [], []
    for i in range(depth):
        j = i // 2
        if i % 2 == 0:
            b_gate, cv = _a_in(x, mixer_norm[i], bf(a_w_in[j]), t)
            x = _a_out(b_gate, cv, cache_conv_a[j], taps(a_w_conv[j]), bf(a_w_out[j]), x, t)
            p, s = _last_rows(cv, bp, sp, bs, ss, ka - 1)
            state_a_p.append(p)
            state_a_s.append(s)
            x = _ffn(x, ffn_norm[i], bf(ffn_w_gate[j]), bf(ffn_w_up[j]), bf(ffn_w_down[j]), t)
        else:
            u = _b_in(x, mixer_norm[i], bf(b_w_pw1[j]), t)
            x = _b_out(u, cache_conv_b[j], taps(b_w_dw[j]), b_dw_bias[j], b_ln_g[j], b_ln_b[j], bf(b_w_pw2[j]), x, t)
            p, s = _last_rows(u, bp, sp, bs, ss, kb - 1)
            state_b_p.append(p)
            state_b_s.append(s)
            x = _moe(x, ffn_norm[i], moe_w_router[j], bf(moe_w_gate[j]), bf(moe_w_up[j]), bf(moe_w_down[j]), t)
    y = _final_norm(x, final_norm, t)
    y_prompt = y[:bp * sp].reshape(bp, sp, d)
    y_sample = y[bp * sp:].reshape(bs, ss, d)
    return (y_prompt, y_sample, jnp.stack(state_a_p), jnp.stack(state_b_p),
            jnp.stack(state_a_s), jnp.stack(state_b_s))


def kernel(x_prompt, x_sample, cache_conv_a, cache_conv_b, mixer_norm, ffn_norm, final_norm, a_w_in, a_w_conv, a_w_out, b_w_pw1, b_w_dw, b_dw_bias, b_ln_g, b_ln_b, b_w_pw2, ffn_w_gate, ffn_w_up, ffn_w_down, moe_w_router, moe_w_gate, moe_w_up, moe_w_down):
    return _forward(x_prompt, x_sample, cache_conv_a, cache_conv_b, mixer_norm, ffn_norm, final_norm,
                    a_w_in, a_w_conv, a_w_out, b_w_pw1, b_w_dw, b_dw_bias, b_ln_g, b_ln_b, b_w_pw2,
                    ffn_w_gate, ffn_w_up, ffn_w_down, moe_w_router, moe_w_gate, moe_w_up, moe_w_down,
                    tm=1024, tn=512, tf=512, te=1024, tc=512)
```

```python
import functools
from typing import NamedTuple

import jax
import jax.numpy as jnp
from jax import lax
from jax.experimental import pallas as pl
from jax.experimental.pallas import tpu as pltpu

RMS_EPS = 1e-6
LN_EPS = 1e-5

V7X_VMEM_LIMIT_BYTES = 56 * 1024 * 1024
LANES = 128
BF16_SUBLANES = 16
I32_1D_TILE = 1024
ISSUE_UNROLL = 8

F32 = jnp.float32
BF16 = jnp.bfloat16
I32 = jnp.int32


class Tiling(NamedTuple):
    tm: int
    n_prompt_tiles: int
    tiles_per_stream: int
    n_sample_tiles: int
    s_streams: int
    s_len: int
    tn: int
    tf: int
    te: int
    tc: int


def _params(n_grid):
    return pltpu.CompilerParams(
        dimension_semantics=("arbitrary",) * n_grid,
        vmem_limit_bytes=V7X_VMEM_LIMIT_BYTES)


def _dot(a, b):
    return jnp.dot(a, b, preferred_element_type=F32)


def _rms(x, g):
    ms = jnp.mean(x * x, axis=-1, keepdims=True)
    return (x * lax.rsqrt(ms + RMS_EPS)) * g


def _sigmoid(x):
    return 1.0 / (1.0 + jnp.exp(-x))


def _token_specs(x, t: "Tiling", cols, col_of):
    if not isinstance(x, tuple):
        return [pl.BlockSpec((t.tm, cols), lambda i, j: (i, col_of(j)))], [x]
    n_p = t.n_prompt_tiles
    return ([pl.BlockSpec((t.tm, cols), lambda i, j: (jnp.minimum(i, n_p - 1), col_of(j))),
             pl.BlockSpec((t.tm, cols), lambda i, j: (jnp.maximum(i - n_p, 0), col_of(j)),
                          pipeline_mode=pl.Buffered(1))], list(x))


def _with_tokens(x_refs, t: "Tiling", fn):
    if len(x_refs) == 1:
        fn(x_refs[0][...])
        return
    i = pl.program_id(0)

    @pl.when(i < t.n_prompt_tiles)
    def _():
        fn(x_refs[0][...])

    @pl.when(i >= t.n_prompt_tiles)
    def _():
        fn(x_refs[1][...])


def _token_shape(x):
    if isinstance(x, tuple):
        return x[0].shape[0] + x[1].shape[0], x[0].shape[1]
    return x.shape


def _conv_tile(src_ref, halo_p_ref, halo_s_ref, w_ref, full_p, full_s, emit, *, t: Tiling, k_width, hp):
    i = pl.program_id(0)
    d = src_ref.shape[1]
    h = k_width - 1

    def run(full_ref, halo_fn, n_s, n_t):
        def chunk(c, carry):
            cols = pl.ds(pl.multiple_of(c * LANES, LANES), LANES)
            full_ref[:, :hp, :] = halo_fn(cols)
            full_ref[:, hp:, :] = src_ref[:, cols].astype(F32).reshape(n_s, n_t, LANES)
            acc = None
            for k in range(k_width):
                lo = hp - h + k
                term = full_ref[:, lo:lo + n_t, :] * w_ref[k:k + 1, cols].reshape(1, 1, LANES)
                acc = term if acc is None else acc + term
            emit(cols, acc.reshape(n_s * n_t, LANES))
            return carry

        lax.fori_loop(0, d // LANES, chunk, 0)

    @pl.when(i < t.n_prompt_tiles)
    def _():
        starts_stream = (i % t.tiles_per_stream) == 0

        def halo(cols):
            v = halo_p_ref[:, cols].astype(F32)
            return jnp.where(starts_stream, jnp.zeros_like(v), v)[None]

        run(full_p, halo, 1, t.tm)

    @pl.when(i >= t.n_prompt_tiles)
    def _():
        run(full_s, lambda cols: halo_s_ref[:, :, cols].astype(F32), t.s_streams, t.s_len)


def _halo_specs(t: Tiling, hp, d):
    blocks_per_tile = t.tm // hp
    prompt = pl.BlockSpec((hp, d), lambda i, j: (jnp.maximum(i * blocks_per_tile - 1, 0), 0))
    sample = pl.BlockSpec((t.s_streams, hp, d),
                          lambda i, j: (jnp.clip(i - t.n_prompt_tiles, 0, t.n_sample_tiles - 1), 0, 0))
    return prompt, sample


def _conv_scratch(t: Tiling, hp):
    return [pltpu.VMEM((1, hp + t.tm, LANES), F32), pltpu.VMEM((t.s_streams, hp + t.s_len, LANES), F32)]


def _pad_cache(cache, hp):
    return jnp.pad(cache.astype(BF16), ((0, 0), (hp - cache.shape[1], 0), (0, 0)))


def _a_in_kernel(*refs, t: Tiling, n_x):
    x_refs, (g_ref, wb_ref, wc_ref, wv_ref, b_out, cv_out, h_scr) = refs[:n_x], refs[n_x:]

    @pl.when(pl.program_id(1) == 0)
    def _():
        def normalise(x):
            h_scr[...] = _rms(x, g_ref[...]).astype(BF16)

        _with_tokens(x_refs, t, normalise)

    h = h_scr[...]
    b_out[...] = _dot(h, wb_ref[...])
    cv_out[...] = (_dot(h, wc_ref[...]) * _dot(h, wv_ref[...])).astype(BF16)


def _a_in(x, g, w_in, layer, t: Tiling):
    m, d = _token_shape(x)
    nj = d // t.tn
    x_specs, x_args = _token_specs(x, t, d, lambda j: 0)
    return pl.pallas_call(
        functools.partial(_a_in_kernel, t=t, n_x=len(x_args)),
        grid=(m // t.tm, nj),
        in_specs=x_specs + [
            pl.BlockSpec((1, d), lambda i, j: (0, 0)),
            pl.BlockSpec((None, d, t.tn), lambda i, j: (layer, 0, j)),
            pl.BlockSpec((None, d, t.tn), lambda i, j: (layer, 0, nj + j)),
            pl.BlockSpec((None, d, t.tn), lambda i, j: (layer, 0, 2 * nj + j)),
        ],
        out_specs=[pl.BlockSpec((t.tm, t.tn), lambda i, j: (i, j))] * 2,
        out_shape=[jax.ShapeDtypeStruct((m, d), F32), jax.ShapeDtypeStruct((m, d), BF16)],
        scratch_shapes=[pltpu.VMEM((t.tm, d), BF16)],
        compiler_params=_params(2),
        name="mixer_a_in",
    )(*x_args, g.reshape(1, d), w_in, w_in, w_in)


def _a_out_kernel(*refs, t: Tiling, n_x, k_width, hp):
    x_refs = refs[:n_x]
    b_ref, cv_ref, halo_p_ref, halo_s_ref, wconv_ref, wout_ref, o_ref, gated_scr, full_p, full_s = refs[n_x:]

    @pl.when(pl.program_id(1) == 0)
    def _():
        def emit(cols, conv):
            gated_scr[:, cols] = (b_ref[:, cols] * conv).astype(BF16)

        _conv_tile(cv_ref, halo_p_ref, halo_s_ref, wconv_ref, full_p, full_s, emit, t=t, k_width=k_width, hp=hp)

    mixed = _dot(gated_scr[...], wout_ref[...])

    def residual(x):
        o_ref[...] = x + mixed

    _with_tokens(x_refs, t, residual)


def _a_out(b, cv, cache, w_conv, w_out, layer, x, t: Tiling):
    m, d = _token_shape(x)
    k_width = w_conv.shape[0]
    hp = BF16_SUBLANES
    halo_p, halo_s = _halo_specs(t, hp, d)
    x_specs, x_args = _token_specs(x, t, t.tn, lambda j: j)
    return pl.pallas_call(
        functools.partial(_a_out_kernel, t=t, n_x=len(x_args), k_width=k_width, hp=hp),
        grid=(m // t.tm, d // t.tn),
        in_specs=x_specs + [
            pl.BlockSpec((t.tm, d), lambda i, j: (i, 0)),
            pl.BlockSpec((t.tm, d), lambda i, j: (i, 0)),
            halo_p, halo_s,
            pl.BlockSpec((k_width, d), lambda i, j: (0, 0)),
            pl.BlockSpec((None, d, t.tn), lambda i, j: (layer, 0, j)),
        ],
        out_specs=pl.BlockSpec((t.tm, t.tn), lambda i, j: (i, j)),
        out_shape=jax.ShapeDtypeStruct((m, d), F32),
        scratch_shapes=[pltpu.VMEM((t.tm, d), BF16)] + _conv_scratch(t, hp),
        compiler_params=_params(2),
        name="mixer_a_out",
    )(*x_args, b, cv, cv, _pad_cache(cache, hp), w_conv, w_out)


def _b_in_kernel(x_ref, g_ref, wa_ref, wg_ref, u_out, h_scr):
    @pl.when(pl.program_id(1) == 0)
    def _():
        h_scr[...] = _rms(x_ref[...], g_ref[...]).astype(BF16)

    h = h_scr[...]
    u_out[...] = (_dot(h, wa_ref[...]) * _sigmoid(_dot(h, wg_ref[...]))).astype(BF16)


def _b_in(x, g, w_pw1, layer, t: Tiling):
    m, d = x.shape
    nj = d // t.tn
    return pl.pallas_call(
        _b_in_kernel,
        grid=(m // t.tm, nj),
        in_specs=[
            pl.BlockSpec((t.tm, d), lambda i, j: (i, 0)),
            pl.BlockSpec((1, d), lambda i, j: (0, 0)),
            pl.BlockSpec((None, d, t.tn), lambda i, j: (layer, 0, j)),
            pl.BlockSpec((None, d, t.tn), lambda i, j: (layer, 0, nj + j)),
        ],
        out_specs=pl.BlockSpec((t.tm, t.tn), lambda i, j: (i, j)),
        out_shape=jax.ShapeDtypeStruct((m, d), BF16),
        scratch_shapes=[pltpu.VMEM((t.tm, d), BF16)],
        compiler_params=_params(2),
        name="mixer_b_in",
    )(x, g.reshape(1, d), w_pw1, w_pw1)


def _b_out_kernel(u_ref, halo_p_ref, halo_s_ref, wdw_ref, bdw_ref, lng_ref, lnb_ref, wpw2_ref, x_ref, o_ref,
                  z_scr, conv_scr, full_p, full_s, *, t: Tiling, k_width, hp):
    @pl.when(pl.program_id(1) == 0)
    def _():
        def emit(cols, conv):
            conv_scr[:, cols] = conv + bdw_ref[:, cols]

        _conv_tile(u_ref, halo_p_ref, halo_s_ref, wdw_ref, full_p, full_s, emit, t=t, k_width=k_width, hp=hp)
        c = conv_scr[...]
        mu = jnp.mean(c, axis=-1, keepdims=True)
        xc = c - mu
        var = jnp.mean(xc * xc, axis=-1, keepdims=True)
        y = (xc * lax.rsqrt(var + LN_EPS)) * lng_ref[...] + lnb_ref[...]
        z_scr[...] = (y * _sigmoid(y)).astype(BF16)

    o_ref[...] = x_ref[...] + _dot(z_scr[...], wpw2_ref[...])


def _b_out(u, cache, w_dw, b_dw, ln_g, ln_b, w_pw2, layer, x, t: Tiling):
    m, d = x.shape
    k_width = w_dw.shape[0]
    hp = -(-(k_width - 1) // BF16_SUBLANES) * BF16_SUBLANES
    halo_p, halo_s = _halo_specs(t, hp, d)
    row = pl.BlockSpec((1, d), lambda i, j: (0, 0))
    return pl.pallas_call(
        functools.partial(_b_out_kernel, t=t, k_width=k_width, hp=hp),
        grid=(m // t.tm, d // t.tn),
        in_specs=[
            pl.BlockSpec((t.tm, d), lambda i, j: (i, 0)),
            halo_p, halo_s,
            pl.BlockSpec((k_width, d), lambda i, j: (0, 0)),
            row, row, row,
            pl.BlockSpec((None, d, t.tn), lambda i, j: (layer, 0, j)),
            pl.BlockSpec((t.tm, t.tn), lambda i, j: (i, j)),
        ],
        out_specs=pl.BlockSpec((t.tm, t.tn), lambda i, j: (i, j)),
        out_shape=jax.ShapeDtypeStruct((m, d), F32),
        scratch_shapes=[pltpu.VMEM((t.tm, d), BF16), pltpu.VMEM((t.tm, d), F32)] + _conv_scratch(t, hp),
        compiler_params=_params(2),
        name="mixer_b_out",
    )(u, u, _pad_cache(cache, hp), w_dw, b_dw.reshape(1, d), ln_g.reshape(1, d), ln_b.reshape(1, d), w_pw2, x)


def _swiglu_step(h, wg_ref, wu_ref, wd_ref):
    gate = _dot(h, wg_ref[...])
    act = (gate * _sigmoid(gate)) * _dot(h, wu_ref[...])
    return _dot(act.astype(BF16), wd_ref[...])


def _ffn_kernel(x_ref, g_ref, wg_ref, wu_ref, wd_ref, o_ref, h_scr):
    @pl.when(pl.program_id(1) == 0)
    def _():
        x = x_ref[...]
        h_scr[...] = _rms(x, g_ref[...]).astype(BF16)
        o_ref[...] = x

    o_ref[...] += _swiglu_step(h_scr[...], wg_ref, wu_ref, wd_ref)


def _ffn(x, g, w_gate, w_up, w_down, layer, t: Tiling):
    m, d = x.shape
    f = w_gate.shape[2]
    return pl.pallas_call(
        _ffn_kernel,
        grid=(m // t.tm, f // t.tf),
        in_specs=[
            pl.BlockSpec((t.tm, d), lambda i, s: (i, 0)),
            pl.BlockSpec((1, d), lambda i, s: (0, 0)),
            pl.BlockSpec((None, d, t.tf), lambda i, s: (layer, 0, s)),
            pl.BlockSpec((None, d, t.tf), lambda i, s: (layer, 0, s)),
            pl.BlockSpec((None, t.tf, d), lambda i, s: (layer, s, 0)),
        ],
        out_specs=pl.BlockSpec((t.tm, d), lambda i, s: (i, 0)),
        out_shape=jax.ShapeDtypeStruct((m, d), F32),
        scratch_shapes=[pltpu.VMEM((t.tm, d), BF16)],
        compiler_params=_params(2),
        name="dense_ffn",
    )(x, g.reshape(1, d), w_gate, w_up, w_down)


def _router_kernel(x_ref, g_ref, wr_ref, meta_ref, wts_ref, cnt_ref, run_scr, *, n_experts):
    @pl.when(pl.program_id(0) == 0)
    def _():
        run_scr[...] = jnp.zeros_like(run_scr)

    h = _rms(x_ref[...], g_ref[...]).astype(BF16)
    logits = _dot(h, wr_ref[...])
    tm = logits.shape[0]
    lane = lax.broadcasted_iota(I32, logits.shape, 1)
    neg = jnp.full_like(logits, -jnp.inf)
    l1 = jnp.where(lane < n_experts, logits, neg)
    m1 = jnp.max(l1, axis=1, keepdims=True)
    i1 = jnp.min(jnp.where(l1 == m1, lane, LANES), axis=1, keepdims=True)
    l2 = jnp.where(lane == i1, neg, l1)
    m2 = jnp.max(l2, axis=1, keepdims=True)
    i2 = jnp.min(jnp.where(l2 == m2, lane, LANES), axis=1, keepdims=True)
    e2 = jnp.exp(m2 - m1)
    w1 = 1.0 / (1.0 + e2)
    w2 = e2 / (1.0 + e2)

    sel = jnp.where((lane == i1) | (lane == i2), 1.0, 0.0)
    row = lax.broadcasted_iota(I32, (tm, tm), 0)
    col = lax.broadcasted_iota(I32, (tm, tm), 1)
    earlier = jnp.where(col < row, 1.0, 0.0).astype(BF16)
    before = _dot(earlier, sel.astype(BF16)) + run_scr[...]
    rank1 = jnp.sum(jnp.where(lane == i1, before, 0.0), axis=1, keepdims=True).astype(I32)
    rank2 = jnp.sum(jnp.where(lane == i2, before, 0.0), axis=1, keepdims=True).astype(I32)

    zero = jnp.zeros_like(lane)
    meta_ref[...] = jnp.where(lane == 0, i1, jnp.where(lane == 1, i2,
                              jnp.where(lane == 2, rank1, jnp.where(lane == 3, rank2, zero))))
    wts_ref[...] = jnp.where(lane == 0, w1, jnp.where(lane == 1, w2, 0.0))
    run_scr[...] += jnp.sum(sel, axis=0, keepdims=True)
    cnt_ref[...] = run_scr[...].astype(I32)


def _router(x, g, w_router, t: Tiling):
    m, d = x.shape
    n_experts = w_router.shape[1]
    wr = jnp.pad(w_router.astype(BF16), ((0, 0), (0, LANES - n_experts)))
    return pl.pallas_call(
        functools.partial(_router_kernel, n_experts=n_experts),
        grid=(m // t.tm,),
        in_specs=[
            pl.BlockSpec((t.tm, d), lambda i: (i, 0)),
            pl.BlockSpec((1, d), lambda i: (0, 0)),
            pl.BlockSpec((d, LANES), lambda i: (0, 0)),
        ],
        out_specs=[pl.BlockSpec((t.tm, LANES), lambda i: (i, 0)), pl.BlockSpec((t.tm, LANES), lambda i: (i, 0)),
                   pl.BlockSpec((1, LANES), lambda i: (0, 0))],
        out_shape=[jax.ShapeDtypeStruct((m, LANES), I32), jax.ShapeDtypeStruct((m, LANES), F32),
                   jax.ShapeDtypeStruct((1, LANES), I32)],
        scratch_shapes=[pltpu.VMEM((1, LANES), F32)],
        compiler_params=_params(1),
        name="moe_router",
    )(x, g.reshape(1, d), wr)


def _row_copy(src_hbm, src_row, dst_ref, dst_row, sem):
    return pltpu.make_async_copy(src_hbm.at[pl.ds(src_row, 1), :], dst_ref.at[pl.ds(dst_row, 1), :], sem)


def _experts_kernel(te_ref, nu_ref, x_hbm, src_hbm, g_ref, wg_ref, wu_ref, wd_ref, o_ref,
                    rows, h_scr, idx_smem, row_sem, idx_sem, *, nf):
    g = pl.program_id(0)
    s = pl.program_id(1)
    n_used = nu_ref[0]
    te = rows.shape[0]
    issue_steps = 1 << ((nf - 1).bit_length() - 1)
    per_step = te // issue_steps

    def idx_copy(tile):
        return pltpu.make_async_copy(src_hbm.at[pl.ds(pl.multiple_of(tile * te, te), te)], idx_smem, idx_sem)

    def issue_rows(lo, n):
        def body(k, carry):
            r = lo + k
            _row_copy(x_hbm, idx_smem[r], rows, r, row_sem).start()
            return carry
        lax.fori_loop(0, n, body, 0, unroll=ISSUE_UNROLL)

    @pl.when(g < n_used)
    def _():
        @pl.when((g == 0) & (s == 0))
        def _():
            idx_copy(0).start()
            idx_copy(0).wait()
            issue_rows(0, te)

        @pl.when(s == 0)
        def _():
            pltpu.make_async_copy(x_hbm.at[pl.ds(0, te), :], rows, row_sem).wait()
            h_scr[...] = _rms(rows[...], g_ref[...]).astype(BF16)
            o_ref[...] = jnp.zeros_like(o_ref)

            @pl.when(g + 1 < n_used)
            def _():
                idx_copy(g + 1).start()

        @pl.when((s >= 1) & (s <= issue_steps) & (g + 1 < n_used))
        def _():
            @pl.when(s == 1)
            def _():
                idx_copy(g + 1).wait()

            issue_rows((s - 1) * per_step, per_step)

        o_ref[...] += _swiglu_step(h_scr[...], wg_ref, wu_ref, wd_ref)

    @pl.when((g >= n_used) & (s == 0))
    def _():
        o_ref[...] = jnp.zeros_like(o_ref)


def _experts(x, g, src_tok, tile_expert, n_used, w_gate, w_up, w_down, layer, t: Tiling):
    m, d = x.shape
    n_tiles = tile_expert.shape[0]
    f = w_gate.shape[3]
    nf = f // t.tf
    assert nf >= 2 and t.te % I32_1D_TILE == 0 and t.te % (1 << ((nf - 1).bit_length() - 1)) == 0

    def w_in_map(gi, s, te_ref, nu_ref):
        return layer, te_ref[gi], 0, jnp.where(gi < nu_ref[0], s, nf - 1)

    def w_out_map(gi, s, te_ref, nu_ref):
        return layer, te_ref[gi], jnp.where(gi < nu_ref[0], s, nf - 1), 0

    return pl.pallas_call(
        functools.partial(_experts_kernel, nf=nf),
        grid_spec=pltpu.PrefetchScalarGridSpec(
            num_scalar_prefetch=2,
            grid=(n_tiles, nf),
            in_specs=[
                pl.BlockSpec(memory_space=pl.ANY),
                pl.BlockSpec(memory_space=pl.ANY),
                pl.BlockSpec((1, d), lambda gi, s, te_ref, nu_ref: (0, 0)),
                pl.BlockSpec((None, None, d, t.tf), w_in_map),
                pl.BlockSpec((None, None, d, t.tf), w_in_map),
                pl.BlockSpec((None, None, t.tf, d), w_out_map),
            ],
            out_specs=pl.BlockSpec((t.te, d), lambda gi, s, te_ref, nu_ref: (gi, 0)),
            scratch_shapes=[
                pltpu.VMEM((t.te, d), F32),
                pltpu.VMEM((t.te, d), BF16),
                pltpu.SMEM((t.te,), I32),
                pltpu.SemaphoreType.DMA(()),
                pltpu.SemaphoreType.DMA(()),
            ]),
        out_shape=jax.ShapeDtypeStruct((n_tiles * t.te, d), F32),
        compiler_params=_params(2),
        name="moe_experts",
    )(tile_expert, n_used, x, src_tok, g.reshape(1, d), w_gate, w_up, w_down)


def _combine_kernel(x_ref, wts_ref, y_hbm, pos_hbm, o_ref, ya, yb, idx_smem, row_sem, idx_sem, *, n_tiles):
    i = pl.program_id(0)
    tc = x_ref.shape[0]
    slot = i % 2

    def idx_copy(tile, sl):
        return pltpu.make_async_copy(pos_hbm.at[pl.ds(pl.multiple_of(tile * (2 * tc), 2 * tc), 2 * tc)],
                                     idx_smem.at[sl], idx_sem.at[sl])

    def issue_rows(sl):
        def body(r, carry):
            _row_copy(y_hbm, idx_smem[sl, r], ya.at[sl], r, row_sem.at[sl]).start()
            _row_copy(y_hbm, idx_smem[sl, tc + r], yb.at[sl], r, row_sem.at[sl]).start()
            return carry
        lax.fori_loop(0, tc, body, 0, unroll=ISSUE_UNROLL)

    @pl.when(i == 0)
    def _():
        idx_copy(0, 0).start()
        idx_copy(0, 0).wait()
        issue_rows(0)

        if n_tiles > 1:
            idx_copy(1, 1).start()

    @pl.when(i + 1 < n_tiles)
    def _():
        idx_copy(i + 1, 1 - slot).wait()
        issue_rows(1 - slot)

    @pl.when(i + 2 < n_tiles)
    def _():
        idx_copy(i + 2, slot).start()

    pltpu.make_async_copy(y_hbm.at[pl.ds(0, tc), :], ya.at[slot], row_sem.at[slot]).wait()
    pltpu.make_async_copy(y_hbm.at[pl.ds(0, tc), :], yb.at[slot], row_sem.at[slot]).wait()
    w = wts_ref[...]
    o_ref[...] = x_ref[...] + w[:, 0:1] * ya[slot] + w[:, 1:2] * yb[slot]


def _combine(x, wts, y, pos_tiles, t: Tiling):
    m, d = x.shape
    n_tiles = m // t.tc
    assert (2 * t.tc) % I32_1D_TILE == 0
    return pl.pallas_call(
        functools.partial(_combine_kernel, n_tiles=n_tiles),
        grid=(n_tiles,),
        in_specs=[
            pl.BlockSpec((t.tc, d), lambda i: (i, 0)),
            pl.BlockSpec((t.tc, LANES), lambda i: (i, 0)),
            pl.BlockSpec(memory_space=pl.ANY),
            pl.BlockSpec(memory_space=pl.ANY),
        ],
        out_specs=pl.BlockSpec((t.tc, d), lambda i: (i, 0)),
        out_shape=jax.ShapeDtypeStruct((m, d), F32),
        scratch_shapes=[
            pltpu.VMEM((2, t.tc, d), F32),
            pltpu.VMEM((2, t.tc, d), F32),
            pltpu.SMEM((2, 2 * t.tc), I32),
            pltpu.SemaphoreType.DMA((2,)),
            pltpu.SemaphoreType.DMA((2,)),
        ],
        compiler_params=_params(1),
        name="moe_combine",
    )(x, wts, y, pos_tiles)


def _moe(x, g, w_router, w_gate, w_up, w_down, layer, t: Tiling):
    m, d = x.shape
    n_experts = w_router.shape[1]
    meta, wts, counts = _router(x, g, w_router, t)

    experts, ranks = meta[:, 0:2], meta[:, 2:4]
    counts = counts[0, :n_experts]
    padded = (counts + t.te - 1) // t.te * t.te
    ends = jnp.cumsum(padded)
    pos = (ends - padded)[experts] + ranks
    n_tiles = 2 * m // t.te + n_experts
    n_used = (ends[-1] // t.te).astype(I32)
    tile_ids = jnp.minimum(jnp.arange(n_tiles, dtype=I32), n_used - 1)
    tile_expert = jnp.minimum(jnp.searchsorted(ends, tile_ids * t.te, side="right"), n_experts - 1).astype(I32)
    tokens = jnp.broadcast_to(jnp.arange(m, dtype=I32)[:, None], (m, 2))
    src_tok = jnp.zeros((n_tiles * t.te,), I32).at[pos.reshape(-1)].set(tokens.reshape(-1), unique_indices=True)
    pos_tiles = pos.reshape(m // t.tc, t.tc, 2).transpose(0, 2, 1).reshape(-1)

    y = _experts(x, g, src_tok, tile_expert, n_used.reshape(1), w_gate, w_up, w_down, layer, t)
    return _combine(x, wts, y, pos_tiles, t)


def _final_norm_kernel(x_ref, g_ref, op_ref, os_ref, *, t: Tiling):
    i = pl.program_id(0)
    y = _rms(x_ref[...], g_ref[...])

    @pl.when(i < t.n_prompt_tiles)
    def _():
        op_ref[...] = y

    @pl.when(i >= t.n_prompt_tiles)
    def _():
        os_ref[...] = y


def _final_norm(x, g, t: Tiling):
    m, d = x.shape
    n_p = t.n_prompt_tiles
    return pl.pallas_call(
        functools.partial(_final_norm_kernel, t=t),
        grid=(m // t.tm,),
        in_specs=[pl.BlockSpec((t.tm, d), lambda i: (i, 0)), pl.BlockSpec((1, d), lambda i: (0, 0))],
        out_specs=[pl.BlockSpec((t.tm, d), lambda i: (jnp.minimum(i, n_p - 1), 0)),
                   pl.BlockSpec((t.tm, d), lambda i: (jnp.maximum(i - n_p, 0), 0))],
        out_shape=[jax.ShapeDtypeStruct((n_p * t.tm, d), F32),
                   jax.ShapeDtypeStruct((t.n_sample_tiles * t.tm, d), F32)],
        compiler_params=_params(1),
        name="final_norm",
    )(x, g.reshape(1, d))


def _tiling(bp, sp, bs, ss, tm, tn, tf, te, tc):
    assert sp % tm == 0 and tm % ss == 0 and (bs * ss) % tm == 0, (sp, ss, bs, tm)
    assert (bp * sp + bs * ss) % tc == 0
    return Tiling(tm=tm, n_prompt_tiles=bp * sp // tm, tiles_per_stream=sp // tm,
                  n_sample_tiles=bs * ss // tm, s_streams=tm // ss, s_len=ss, tn=tn, tf=tf, te=te, tc=tc)


def _last_rows(v, bp, sp, bs, ss, n):
    d = v.shape[1]
    p = v[:bp * sp].reshape(bp, sp, d)[:, sp - n:].astype(F32)
    s = v[bp * sp:].reshape(bs, ss, d)[:, ss - n:].astype(F32)
    return p, s


def _forward(x_prompt, x_sample, cache_conv_a, cache_conv_b, mixer_norm, ffn_norm, final_norm,
             a_w_in, a_w_conv, a_w_out, b_w_pw1, b_w_dw, b_dw_bias, b_ln_g, b_ln_b, b_w_pw2,
             ffn_w_gate, ffn_w_up, ffn_w_down, moe_w_router, moe_w_gate, moe_w_up, moe_w_down,
             *, tm, tn, tf, te, tc):
    bp, sp, d = x_prompt.shape
    bs, ss, _ = x_sample.shape
    depth = mixer_norm.shape[0]
    t = _tiling(bp, sp, bs, ss, tm, tn, tf, te, tc)
    ka, kb = a_w_conv.shape[1], b_w_dw.shape[1]

    x = (x_prompt.reshape(bp * sp, d), x_sample.reshape(bs * ss, d))
    a_w_in, a_w_out, b_w_pw1, b_w_pw2 = (w.astype(BF16) for w in (a_w_in, a_w_out, b_w_pw1, b_w_pw2))
    ffn_w_gate, ffn_w_up, ffn_w_down = (w.astype(BF16) for w in (ffn_w_gate, ffn_w_up, ffn_w_down))
    moe_w_gate, moe_w_up, moe_w_down = (w.astype(BF16) for w in (moe_w_gate, moe_w_up, moe_w_down))
    taps = lambda w: w.astype(BF16).astype(F32)
    state_a_p, state_a_s, state_b_p, state_b_s = [], [], [], []
    for i in range(depth):
        j = i // 2
        if i % 2 == 0:
            b_gate, cv = _a_in(x, mixer_norm[i], a_w_in, j, t)
            x = _a_out(b_gate, cv, cache_conv_a[j], taps(a_w_conv[j]), a_w_out, j, x, t)
            p, s = _last_rows(cv, bp, sp, bs, ss, ka - 1)
            state_a_p.append(p)
            state_a_s.append(s)
            x = _ffn(x, ffn_norm[i], ffn_w_gate, ffn_w_up, ffn_w_down, j, t)
        else:
            u = _b_in(x, mixer_norm[i], b_w_pw1, j, t)
            x = _b_out(u, cache_conv_b[j], taps(b_w_dw[j]), b_dw_bias[j], b_ln_g[j], b_ln_b[j], b_w_pw2, j, x, t)
            p, s = _last_rows(u, bp, sp, bs, ss, kb - 1)
            state_b_p.append(p)
            state_b_s.append(s)
            x = _moe(x, ffn_norm[i], moe_w_router[j], moe_w_gate, moe_w_up, moe_w_down, j, t)
    y_prompt, y_sample = _final_norm(x, final_norm, t)
    return (y_prompt.reshape(bp, sp, d), y_sample.reshape(bs, ss, d), jnp.stack(state_a_p), jnp.stack(state_b_p),
            jnp.stack(state_a_s), jnp.stack(state_b_s))


def kernel(x_prompt, x_sample, cache_conv_a, cache_conv_b, mixer_norm, ffn_norm, final_norm, a_w_in, a_w_conv, a_w_out, b_w_pw1, b_w_dw, b_dw_bias, b_ln_g, b_ln_b, b_w_pw2, ffn_w_gate, ffn_w_up, ffn_w_down, moe_w_router, moe_w_gate, moe_w_up, moe_w_down):
    return _forward(x_prompt, x_sample, cache_conv_a, cache_conv_b, mixer_norm, ffn_norm, final_norm,
                    a_w_in, a_w_conv, a_w_out, b_w_pw1, b_w_dw, b_dw_bias, b_ln_g, b_ln_b, b_w_pw2,
                    ffn_w_gate, ffn_w_up, ffn_w_down, moe_w_router, moe_w_gate, moe_w_up, moe_w_down,
                    tm=1024, tn=512, tf=512, te=1024, tc=512)
```

```python
import functools
from typing import NamedTuple

import jax
import jax.numpy as jnp
from jax import lax
from jax.experimental import pallas as pl
from jax.experimental.pallas import tpu as pltpu

RMS_EPS = 1e-6
LN_EPS = 1e-5

V7X_VMEM_LIMIT_BYTES = 56 * 1024 * 1024
LANES = 128
BF16_SUBLANES = 16
I32_1D_TILE = 1024
ISSUE_UNROLL = 8

F32 = jnp.float32
BF16 = jnp.bfloat16
I32 = jnp.int32


class Tiling(NamedTuple):
    tm: int
    n_prompt_tiles: int
    tiles_per_stream: int
    n_sample_tiles: int
    s_streams: int
    s_len: int
    tn: int
    tf: int
    te: int
    tc: int


def _params(n_grid):
    return pltpu.CompilerParams(
        dimension_semantics=("arbitrary",) * n_grid,
        vmem_limit_bytes=V7X_VMEM_LIMIT_BYTES)


def _dot(a, b):
    return jnp.dot(a, b, preferred_element_type=F32)


def _rms(x, g):
    ms = jnp.mean(x * x, axis=-1, keepdims=True)
    return (x * lax.rsqrt(ms + RMS_EPS)) * g


def _sigmoid(x):
    return 1.0 / (1.0 + jnp.exp(-x))


def _token_specs(x, t: "Tiling", cols, col_of):
    if not isinstance(x, tuple):
        return [pl.BlockSpec((t.tm, cols), lambda i, j: (i, col_of(j)))], [x]
    n_p = t.n_prompt_tiles
    return ([pl.BlockSpec((t.tm, cols), lambda i, j: (jnp.minimum(i, n_p - 1), col_of(j))),
             pl.BlockSpec((t.tm, cols), lambda i, j: (jnp.maximum(i - n_p, 0), jnp.where(i < n_p, 0, col_of(j))),
                          pipeline_mode=pl.Buffered(1))], list(x))


def _with_tokens(x_refs, t: "Tiling", fn):
    if len(x_refs) == 1:
        fn(x_refs[0][...])
        return
    i = pl.program_id(0)

    @pl.when(i < t.n_prompt_tiles)
    def _():
        fn(x_refs[0][...])

    @pl.when(i >= t.n_prompt_tiles)
    def _():
        fn(x_refs[1][...])


def _token_shape(x):
    if isinstance(x, tuple):
        return x[0].shape[0] + x[1].shape[0], x[0].shape[1]
    return x.shape


def _conv_tile(src_ref, halo_p_ref, halo_s_ref, w_ref, full_p, full_s, emit, *, t: Tiling, k_width, hp):
    i = pl.program_id(0)
    d = src_ref.shape[1]
    h = k_width - 1

    def run(full_ref, halo_fn, n_s, n_t):
        def chunk(c, carry):
            cols = pl.ds(pl.multiple_of(c * LANES, LANES), LANES)
            full_ref[:, :hp, :] = halo_fn(cols)
            full_ref[:, hp:, :] = src_ref[:, cols].astype(F32).reshape(n_s, n_t, LANES)
            acc = None
            for k in range(k_width):
                lo = hp - h + k
                term = full_ref[:, lo:lo + n_t, :] * w_ref[k:k + 1, cols].reshape(1, 1, LANES)
                acc = term if acc is None else acc + term
            emit(cols, acc.reshape(n_s * n_t, LANES))
            return carry

        lax.fori_loop(0, d // LANES, chunk, 0)

    @pl.when(i < t.n_prompt_tiles)
    def _():
        starts_stream = (i % t.tiles_per_stream) == 0

        def halo(cols):
            v = halo_p_ref[:, cols].astype(F32)
            return jnp.where(starts_stream, jnp.zeros_like(v), v)[None]

        run(full_p, halo, 1, t.tm)

    @pl.when(i >= t.n_prompt_tiles)
    def _():
        run(full_s, lambda cols: halo_s_ref[:, :, cols].astype(F32), t.s_streams, t.s_len)


def _halo_specs(t: Tiling, hp, d):
    blocks_per_tile = t.tm // hp
    prompt = pl.BlockSpec((hp, d), lambda i, j: (jnp.maximum(i * blocks_per_tile - 1, 0), 0))
    sample = pl.BlockSpec((t.s_streams, hp, d),
                          lambda i, j: (jnp.clip(i - t.n_prompt_tiles, 0, t.n_sample_tiles - 1), 0, 0))
    return prompt, sample


def _conv_scratch(t: Tiling, hp):
    return [pltpu.VMEM((1, hp + t.tm, LANES), F32), pltpu.VMEM((t.s_streams, hp + t.s_len, LANES), F32)]


def _pad_cache(cache, hp):
    return jnp.pad(cache.astype(BF16), ((0, 0), (hp - cache.shape[1], 0), (0, 0)))


def _a_in_kernel(*refs, t: Tiling, n_x):
    x_refs, (g_ref, wb_ref, wc_ref, wv_ref, b_out, cv_out, h_scr) = refs[:n_x], refs[n_x:]

    @pl.when(pl.program_id(1) == 0)
    def _():
        def normalise(x):
            h_scr[...] = _rms(x, g_ref[...]).astype(BF16)

        _with_tokens(x_refs, t, normalise)

    h = h_scr[...]
    b_out[...] = _dot(h, wb_ref[...])
    cv_out[...] = (_dot(h, wc_ref[...]) * _dot(h, wv_ref[...])).astype(BF16)


def _a_in(x, g, w_in, layer, t: Tiling):
    m, d = _token_shape(x)
    nj = d // t.tn
    x_specs, x_args = _token_specs(x, t, d, lambda j: 0)
    return pl.pallas_call(
        functools.partial(_a_in_kernel, t=t, n_x=len(x_args)),
        grid=(m // t.tm, nj),
        in_specs=x_specs + [
            pl.BlockSpec((1, d), lambda i, j: (0, 0)),
            pl.BlockSpec((None, d, t.tn), lambda i, j: (layer, 0, j)),
            pl.BlockSpec((None, d, t.tn), lambda i, j: (layer, 0, nj + j)),
            pl.BlockSpec((None, d, t.tn), lambda i, j: (layer, 0, 2 * nj + j)),
        ],
        out_specs=[pl.BlockSpec((t.tm, t.tn), lambda i, j: (i, j))] * 2,
        out_shape=[jax.ShapeDtypeStruct((m, d), F32), jax.ShapeDtypeStruct((m, d), BF16)],
        scratch_shapes=[pltpu.VMEM((t.tm, d), BF16)],
        compiler_params=_params(2),
        name="mixer_a_in",
    )(*x_args, g.reshape(1, d), w_in, w_in, w_in)


def _a_out_kernel(*refs, t: Tiling, n_x, k_width, hp):
    x_refs = refs[:n_x]
    b_ref, cv_ref, halo_p_ref, halo_s_ref, wconv_ref, wout_ref, o_ref, gated_scr, full_p, full_s = refs[n_x:]

    @pl.when(pl.program_id(1) == 0)
    def _():
        def emit(cols, conv):
            gated_scr[:, cols] = (b_ref[:, cols] * conv).astype(BF16)

        _conv_tile(cv_ref, halo_p_ref, halo_s_ref, wconv_ref, full_p, full_s, emit, t=t, k_width=k_width, hp=hp)

    mixed = _dot(gated_scr[...], wout_ref[...])

    def residual(x):
        o_ref[...] = x + mixed

    _with_tokens(x_refs, t, residual)


def _a_out(b, cv, cache, w_conv, w_out, layer, x, t: Tiling):
    m, d = _token_shape(x)
    k_width = w_conv.shape[0]
    hp = BF16_SUBLANES
    halo_p, halo_s = _halo_specs(t, hp, d)
    x_specs, x_args = _token_specs(x, t, t.tn, lambda j: j)
    return pl.pallas_call(
        functools.partial(_a_out_kernel, t=t, n_x=len(x_args), k_width=k_width, hp=hp),
        grid=(m // t.tm, d // t.tn),
        in_specs=x_specs + [
            pl.BlockSpec((t.tm, d), lambda i, j: (i, 0)),
            pl.BlockSpec((t.tm, d), lambda i, j: (i, 0)),
            halo_p, halo_s,
            pl.BlockSpec((k_width, d), lambda i, j: (0, 0)),
            pl.BlockSpec((None, d, t.tn), lambda i, j: (layer, 0, j)),
        ],
        out_specs=pl.BlockSpec((t.tm, t.tn), lambda i, j: (i, j)),
        out_shape=jax.ShapeDtypeStruct((m, d), F32),
        scratch_shapes=[pltpu.VMEM((t.tm, d), BF16)] + _conv_scratch(t, hp),
        compiler_params=_params(2),
        name="mixer_a_out",
    )(*x_args, b, cv, cv, _pad_cache(cache, hp), w_conv, w_out)


def _b_in_kernel(x_ref, g_ref, wa_ref, wg_ref, u_out, h_scr):
    @pl.when(pl.program_id(1) == 0)
    def _():
        h_scr[...] = _rms(x_ref[...], g_ref[...]).astype(BF16)

    h = h_scr[...]
    u_out[...] = (_dot(h, wa_ref[...]) * _sigmoid(_dot(h, wg_ref[...]))).astype(BF16)


def _b_in(x, g, w_pw1, layer, t: Tiling):
    m, d = x.shape
    nj = d // t.tn
    return pl.pallas_call(
        _b_in_kernel,
        grid=(m // t.tm, nj),
        in_specs=[
            pl.BlockSpec((t.tm, d), lambda i, j: (i, 0)),
            pl.BlockSpec((1, d), lambda i, j: (0, 0)),
            pl.BlockSpec((None, d, t.tn), lambda i, j: (layer, 0, j)),
            pl.BlockSpec((None, d, t.tn), lambda i, j: (layer, 0, nj + j)),
        ],
        out_specs=pl.BlockSpec((t.tm, t.tn), lambda i, j: (i, j)),
        out_shape=jax.ShapeDtypeStruct((m, d), BF16),
        scratch_shapes=[pltpu.VMEM((t.tm, d), BF16)],
        compiler_params=_params(2),
        name="mixer_b_in",
    )(x, g.reshape(1, d), w_pw1, w_pw1)


def _b_out_kernel(u_ref, halo_p_ref, halo_s_ref, wdw_ref, bdw_ref, lng_ref, lnb_ref, wpw2_ref, x_ref, o_ref,
                  z_scr, conv_scr, full_p, full_s, *, t: Tiling, k_width, hp):
    @pl.when(pl.program_id(1) == 0)
    def _():
        def emit(cols, conv):
            conv_scr[:, cols] = conv + bdw_ref[:, cols]

        _conv_tile(u_ref, halo_p_ref, halo_s_ref, wdw_ref, full_p, full_s, emit, t=t, k_width=k_width, hp=hp)
        c = conv_scr[...]
        mu = jnp.mean(c, axis=-1, keepdims=True)
        xc = c - mu
        var = jnp.mean(xc * xc, axis=-1, keepdims=True)
        y = (xc * lax.rsqrt(var + LN_EPS)) * lng_ref[...] + lnb_ref[...]
        z_scr[...] = (y * _sigmoid(y)).astype(BF16)

    o_ref[...] = x_ref[...] + _dot(z_scr[...], wpw2_ref[...])


def _b_out(u, cache, w_dw, b_dw, ln_g, ln_b, w_pw2, layer, x, t: Tiling):
    m, d = x.shape
    k_width = w_dw.shape[0]
    hp = -(-(k_width - 1) // BF16_SUBLANES) * BF16_SUBLANES
    halo_p, halo_s = _halo_specs(t, hp, d)
    row = pl.BlockSpec((1, d), lambda i, j: (0, 0))
    return pl.pallas_call(
        functools.partial(_b_out_kernel, t=t, k_width=k_width, hp=hp),
        grid=(m // t.tm, d // t.tn),
        in_specs=[
            pl.BlockSpec((t.tm, d), lambda i, j: (i, 0)),
            halo_p, halo_s,
            pl.BlockSpec((k_width, d), lambda i, j: (0, 0)),
            row, row, row,
            pl.BlockSpec((None, d, t.tn), lambda i, j: (layer, 0, j)),
            pl.BlockSpec((t.tm, t.tn), lambda i, j: (i, j)),
        ],
        out_specs=pl.BlockSpec((t.tm, t.tn), lambda i, j: (i, j)),
        out_shape=jax.ShapeDtypeStruct((m, d), F32),
        scratch_shapes=[pltpu.VMEM((t.tm, d), BF16), pltpu.VMEM((t.tm, d), F32)] + _conv_scratch(t, hp),
        compiler_params=_params(2),
        name="mixer_b_out",
    )(u, u, _pad_cache(cache, hp), w_dw, b_dw.reshape(1, d), ln_g.reshape(1, d), ln_b.reshape(1, d), w_pw2, x)


def _swiglu_step(h, wg_ref, wu_ref, wd_ref):
    gate = _dot(h, wg_ref[...])
    act = (gate * _sigmoid(gate)) * _dot(h, wu_ref[...])
    return _dot(act.astype(BF16), wd_ref[...])


def _ffn_kernel(x_ref, g_ref, wg_ref, wu_ref, wd_ref, o_ref, h_scr):
    @pl.when(pl.program_id(1) == 0)
    def _():
        x = x_ref[...]
        h_scr[...] = _rms(x, g_ref[...]).astype(BF16)
        o_ref[...] = x

    o_ref[...] += _swiglu_step(h_scr[...], wg_ref, wu_ref, wd_ref)


def _ffn(x, g, w_gate, w_up, w_down, layer, t: Tiling):
    m, d = x.shape
    f = w_gate.shape[2]
    return pl.pallas_call(
        _ffn_kernel,
        grid=(m // t.tm, f // t.tf),
        in_specs=[
            pl.BlockSpec((t.tm, d), lambda i, s: (i, 0)),
            pl.BlockSpec((1, d), lambda i, s: (0, 0)),
            pl.BlockSpec((None, d, t.tf), lambda i, s: (layer, 0, s)),
            pl.BlockSpec((None, d, t.tf), lambda i, s: (layer, 0, s)),
            pl.BlockSpec((None, t.tf, d), lambda i, s: (layer, s, 0)),
        ],
        out_specs=pl.BlockSpec((t.tm, d), lambda i, s: (i, 0)),
        out_shape=jax.ShapeDtypeStruct((m, d), F32),
        scratch_shapes=[pltpu.VMEM((t.tm, d), BF16)],
        compiler_params=_params(2),
        name="dense_ffn",
    )(x, g.reshape(1, d), w_gate, w_up, w_down)


def _router_kernel(x_ref, g_ref, wr_ref, meta_ref, wts_ref, cnt_ref, run_scr, *, n_experts):
    @pl.when(pl.program_id(0) == 0)
    def _():
        run_scr[...] = jnp.zeros_like(run_scr)

    h = _rms(x_ref[...], g_ref[...]).astype(BF16)
    logits = _dot(h, wr_ref[...])
    tm = logits.shape[0]
    lane = lax.broadcasted_iota(I32, logits.shape, 1)
    neg = jnp.full_like(logits, -jnp.inf)
    l1 = jnp.where(lane < n_experts, logits, neg)
    m1 = jnp.max(l1, axis=1, keepdims=True)
    i1 = jnp.min(jnp.where(l1 == m1, lane, LANES), axis=1, keepdims=True)
    l2 = jnp.where(lane == i1, neg, l1)
    m2 = jnp.max(l2, axis=1, keepdims=True)
    i2 = jnp.min(jnp.where(l2 == m2, lane, LANES), axis=1, keepdims=True)
    e2 = jnp.exp(m2 - m1)
    w1 = 1.0 / (1.0 + e2)
    w2 = e2 / (1.0 + e2)

    sel = jnp.where((lane == i1) | (lane == i2), 1.0, 0.0)
    row = lax.broadcasted_iota(I32, (tm, tm), 0)
    col = lax.broadcasted_iota(I32, (tm, tm), 1)
    earlier = jnp.where(col < row, 1.0, 0.0).astype(BF16)
    before = _dot(earlier, sel.astype(BF16)) + run_scr[...]
    rank1 = jnp.sum(jnp.where(lane == i1, before, 0.0), axis=1, keepdims=True).astype(I32)
    rank2 = jnp.sum(jnp.where(lane == i2, before, 0.0), axis=1, keepdims=True).astype(I32)

    zero = jnp.zeros_like(lane)
    meta_ref[...] = jnp.where(lane == 0, i1, jnp.where(lane == 1, i2,
                              jnp.where(lane == 2, rank1, jnp.where(lane == 3, rank2, zero))))
    wts_ref[...] = jnp.where(lane == 0, w1, jnp.where(lane == 1, w2, 0.0))
    run_scr[...] += jnp.sum(sel, axis=0, keepdims=True)
    cnt_ref[...] = run_scr[...].astype(I32)


def _router(x, g, w_router, t: Tiling):
    m, d = x.shape
    n_experts = w_router.shape[1]
    wr = jnp.pad(w_router.astype(BF16), ((0, 0), (0, LANES - n_experts)))
    return pl.pallas_call(
        functools.partial(_router_kernel, n_experts=n_experts),
        grid=(m // t.tm,),
        in_specs=[
            pl.BlockSpec((t.tm, d), lambda i: (i, 0)),
            pl.BlockSpec((1, d), lambda i: (0, 0)),
            pl.BlockSpec((d, LANES), lambda i: (0, 0)),
        ],
        out_specs=[pl.BlockSpec((t.tm, LANES), lambda i: (i, 0)), pl.BlockSpec((t.tm, LANES), lambda i: (i, 0)),
                   pl.BlockSpec((1, LANES), lambda i: (0, 0))],
        out_shape=[jax.ShapeDtypeStruct((m, LANES), I32), jax.ShapeDtypeStruct((m, LANES), F32),
                   jax.ShapeDtypeStruct((1, LANES), I32)],
        scratch_shapes=[pltpu.VMEM((1, LANES), F32)],
        compiler_params=_params(1),
        name="moe_router",
    )(x, g.reshape(1, d), wr)


def _row_copy(src_hbm, src_row, dst_ref, dst_row, sem):
    return pltpu.make_async_copy(src_hbm.at[pl.ds(src_row, 1), :], dst_ref.at[pl.ds(dst_row, 1), :], sem)


def _experts_kernel(te_ref, nu_ref, x_hbm, src_hbm, g_ref, wg_ref, wu_ref, wd_ref, o_ref,
                    rows, h_scr, idx_smem, row_sem, idx_sem, *, nf):
    g = pl.program_id(0)
    s = pl.program_id(1)
    n_used = nu_ref[0]
    te = rows.shape[0]
    issue_steps = 1 << ((nf - 1).bit_length() - 1)
    per_step = te // issue_steps

    def idx_copy(tile):
        return pltpu.make_async_copy(src_hbm.at[pl.ds(pl.multiple_of(tile * te, te), te)], idx_smem, idx_sem)

    def issue_rows(lo, n):
        def body(k, carry):
            r = lo + k
            _row_copy(x_hbm, idx_smem[r], rows, r, row_sem).start()
            return carry
        lax.fori_loop(0, n, body, 0, unroll=ISSUE_UNROLL)

    @pl.when(g < n_used)
    def _():
        @pl.when((g == 0) & (s == 0))
        def _():
            idx_copy(0).start()
            idx_copy(0).wait()
            issue_rows(0, te)

        @pl.when(s == 0)
        def _():
            pltpu.make_async_copy(x_hbm.at[pl.ds(0, te), :], rows, row_sem).wait()
            h_scr[...] = _rms(rows[...], g_ref[...]).astype(BF16)
            o_ref[...] = jnp.zeros_like(o_ref)

            @pl.when(g + 1 < n_used)
            def _():
                idx_copy(g + 1).start()

        @pl.when((s >= 1) & (s <= issue_steps) & (g + 1 < n_used))
        def _():
            @pl.when(s == 1)
            def _():
                idx_copy(g + 1).wait()

            issue_rows((s - 1) * per_step, per_step)

        o_ref[...] += _swiglu_step(h_scr[...], wg_ref, wu_ref, wd_ref)

    @pl.when((g >= n_used) & (s == 0))
    def _():
        o_ref[...] = jnp.zeros_like(o_ref)


def _experts(x, g, src_tok, tile_expert, n_used, w_gate, w_up, w_down, layer, t: Tiling):
    m, d = x.shape
    n_tiles = tile_expert.shape[0]
    f = w_gate.shape[3]
    nf = f // t.tf
    assert nf >= 2 and t.te % I32_1D_TILE == 0 and t.te % (1 << ((nf - 1).bit_length() - 1)) == 0

    def w_in_map(gi, s, te_ref, nu_ref):
        return layer, te_ref[gi], 0, jnp.where(gi < nu_ref[0], s, nf - 1)

    def w_out_map(gi, s, te_ref, nu_ref):
        return layer, te_ref[gi], jnp.where(gi < nu_ref[0], s, nf - 1), 0

    return pl.pallas_call(
        functools.partial(_experts_kernel, nf=nf),
        grid_spec=pltpu.PrefetchScalarGridSpec(
            num_scalar_prefetch=2,
            grid=(n_tiles, nf),
            in_specs=[
                pl.BlockSpec(memory_space=pl.ANY),
                pl.BlockSpec(memory_space=pl.ANY),
                pl.BlockSpec((1, d), lambda gi, s, te_ref, nu_ref: (0, 0)),
                pl.BlockSpec((None, None, d, t.tf), w_in_map),
                pl.BlockSpec((None, None, d, t.tf), w_in_map),
                pl.BlockSpec((None, None, t.tf, d), w_out_map),
            ],
            out_specs=pl.BlockSpec((t.te, d), lambda gi, s, te_ref, nu_ref: (gi, 0)),
            scratch_shapes=[
                pltpu.VMEM((t.te, d), F32),
                pltpu.VMEM((t.te, d), BF16),
                pltpu.SMEM((t.te,), I32),
                pltpu.SemaphoreType.DMA(()),
                pltpu.SemaphoreType.DMA(()),
            ]),
        out_shape=jax.ShapeDtypeStruct((n_tiles * t.te, d), F32),
        compiler_params=_params(2),
        name="moe_experts",
    )(tile_expert, n_used, x, src_tok, g.reshape(1, d), w_gate, w_up, w_down)


def _combine_kernel(x_ref, wts_ref, y_hbm, pos_hbm, o_ref, ya, yb, idx_smem, row_sem, idx_sem, *, n_tiles):
    i = pl.program_id(0)
    tc = x_ref.shape[0]
    slot = i % 2

    def idx_copy(tile, sl):
        return pltpu.make_async_copy(pos_hbm.at[pl.ds(pl.multiple_of(tile * (2 * tc), 2 * tc), 2 * tc)],
                                     idx_smem.at[sl], idx_sem.at[sl])

    def issue_rows(sl):
        def body(r, carry):
            _row_copy(y_hbm, idx_smem[sl, r], ya.at[sl], r, row_sem.at[sl]).start()
            _row_copy(y_hbm, idx_smem[sl, tc + r], yb.at[sl], r, row_sem.at[sl]).start()
            return carry
        lax.fori_loop(0, tc, body, 0, unroll=ISSUE_UNROLL)

    @pl.when(i == 0)
    def _():
        idx_copy(0, 0).start()
        idx_copy(0, 0).wait()
        issue_rows(0)

        if n_tiles > 1:
            idx_copy(1, 1).start()

    @pl.when(i + 1 < n_tiles)
    def _():
        idx_copy(i + 1, 1 - slot).wait()
        issue_rows(1 - slot)

    @pl.when(i + 2 < n_tiles)
    def _():
        idx_copy(i + 2, slot).start()

    pltpu.make_async_copy(y_hbm.at[pl.ds(0, tc), :], ya.at[slot], row_sem.at[slot]).wait()
    pltpu.make_async_copy(y_hbm.at[pl.ds(0, tc), :], yb.at[slot], row_sem.at[slot]).wait()
    w = wts_ref[...]
    o_ref[...] = x_ref[...] + w[:, 0:1] * ya[slot] + w[:, 1:2] * yb[slot]


def _combine(x, wts, y, pos_tiles, t: Tiling):
    m, d = x.shape
    n_tiles = m // t.tc
    assert (2 * t.tc) % I32_1D_TILE == 0
    return pl.pallas_call(
        functools.partial(_combine_kernel, n_tiles=n_tiles),
        grid=(n_tiles,),
        in_specs=[
            pl.BlockSpec((t.tc, d), lambda i: (i, 0)),
            pl.BlockSpec((t.tc, LANES), lambda i: (i, 0)),
            pl.BlockSpec(memory_space=pl.ANY),
            pl.BlockSpec(memory_space=pl.ANY),
        ],
        out_specs=pl.BlockSpec((t.tc, d), lambda i: (i, 0)),
        out_shape=jax.ShapeDtypeStruct((m, d), F32),
        scratch_shapes=[
            pltpu.VMEM((2, t.tc, d), F32),
            pltpu.VMEM((2, t.tc, d), F32),
            pltpu.SMEM((2, 2 * t.tc), I32),
            pltpu.SemaphoreType.DMA((2,)),
            pltpu.SemaphoreType.DMA((2,)),
        ],
        compiler_params=_params(1),
        name="moe_combine",
    )(x, wts, y, pos_tiles)


def _moe(x, g, w_router, w_gate, w_up, w_down, layer, t: Tiling):
    m, d = x.shape
    n_experts = w_router.shape[1]
    meta, wts, counts = _router(x, g, w_router, t)

    experts, ranks = meta[:, 0:2], meta[:, 2:4]
    counts = counts[0, :n_experts]
    padded = (counts + t.te - 1) // t.te * t.te
    ends = jnp.cumsum(padded)
    pos = (ends - padded)[experts] + ranks
    n_tiles = 2 * m // t.te + n_experts
    n_used = (ends[-1] // t.te).astype(I32)
    tile_ids = jnp.minimum(jnp.arange(n_tiles, dtype=I32), n_used - 1)
    tile_expert = jnp.minimum(jnp.searchsorted(ends, tile_ids * t.te, side="right"), n_experts - 1).astype(I32)
    tokens = jnp.broadcast_to(jnp.arange(m, dtype=I32)[:, None], (m, 2))
    src_tok = jnp.zeros((n_tiles * t.te,), I32).at[pos.reshape(-1)].set(tokens.reshape(-1), unique_indices=True)
    pos_tiles = pos.reshape(m // t.tc, t.tc, 2).transpose(0, 2, 1).reshape(-1)

    y = _experts(x, g, src_tok, tile_expert, n_used.reshape(1), w_gate, w_up, w_down, layer, t)
    return _combine(x, wts, y, pos_tiles, t)


def _final_norm_kernel(x_ref, g_ref, op_ref, os_ref, *, t: Tiling):
    i = pl.program_id(0)
    y = _rms(x_ref[...], g_ref[...])

    @pl.when(i < t.n_prompt_tiles)
    def _():
        op_ref[...] = y

    @pl.when(i >= t.n_prompt_tiles)
    def _():
        os_ref[...] = y


def _final_norm(x, g, t: Tiling):
    m, d = x.shape
    n_p = t.n_prompt_tiles
    return pl.pallas_call(
        functools.partial(_final_norm_kernel, t=t),
        grid=(m // t.tm,),
        in_specs=[pl.BlockSpec((t.tm, d), lambda i: (i, 0)), pl.BlockSpec((1, d), lambda i: (0, 0))],
        out_specs=[pl.BlockSpec((t.tm, d), lambda i: (jnp.minimum(i, n_p - 1), 0)),
                   pl.BlockSpec((t.tm, d), lambda i: (jnp.maximum(i - n_p, 0), 0))],
        out_shape=[jax.ShapeDtypeStruct((n_p * t.tm, d), F32),
                   jax.ShapeDtypeStruct((t.n_sample_tiles * t.tm, d), F32)],
        compiler_params=_params(1),
        name="final_norm",
    )(x, g.reshape(1, d))


def _tiling(bp, sp, bs, ss, tm, tn, tf, te, tc):
    assert sp % tm == 0 and tm % ss == 0 and (bs * ss) % tm == 0, (sp, ss, bs, tm)
    assert (bp * sp + bs * ss) % tc == 0
    return Tiling(tm=tm, n_prompt_tiles=bp * sp // tm, tiles_per_stream=sp // tm,
                  n_sample_tiles=bs * ss // tm, s_streams=tm // ss, s_len=ss, tn=tn, tf=tf, te=te, tc=tc)


def _last_rows(v, bp, sp, bs, ss, n):
    d = v.shape[1]
    tail = jnp.arange(n, dtype=I32)
    rows_p = (jnp.arange(bp, dtype=I32) * sp + (sp - n))[:, None] + tail[None, :]
    rows_s = bp * sp + (jnp.arange(bs, dtype=I32) * ss + (ss - n))[:, None] + tail[None, :]
    p = jnp.take(v, rows_p.reshape(-1), axis=0).reshape(bp, n, d).astype(F32)
    s = jnp.take(v, rows_s.reshape(-1), axis=0).reshape(bs, n, d).astype(F32)
    return p, s


def _forward(x_prompt, x_sample, cache_conv_a, cache_conv_b, mixer_norm, ffn_norm, final_norm,
             a_w_in, a_w_conv, a_w_out, b_w_pw1, b_w_dw, b_dw_bias, b_ln_g, b_ln_b, b_w_pw2,
             ffn_w_gate, ffn_w_up, ffn_w_down, moe_w_router, moe_w_gate, moe_w_up, moe_w_down,
             *, tm, tn, tf, te, tc):
    bp, sp, d = x_prompt.shape
    bs, ss, _ = x_sample.shape
    depth = mixer_norm.shape[0]
    t = _tiling(bp, sp, bs, ss, tm, tn, tf, te, tc)
    ka, kb = a_w_conv.shape[1], b_w_dw.shape[1]

    x = (x_prompt.reshape(bp * sp, d), x_sample.reshape(bs * ss, d))
    a_w_in, a_w_out, b_w_pw1, b_w_pw2 = (w.astype(BF16) for w in (a_w_in, a_w_out, b_w_pw1, b_w_pw2))
    ffn_w_gate, ffn_w_up, ffn_w_down = (w.astype(BF16) for w in (ffn_w_gate, ffn_w_up, ffn_w_down))
    moe_w_gate, moe_w_up, moe_w_down = (w.astype(BF16) for w in (moe_w_gate, moe_w_up, moe_w_down))
    taps = lambda w: w.astype(BF16).astype(F32)
    state_a_p, state_a_s, state_b_p, state_b_s = [], [], [], []
    for i in range(depth):
        j = i // 2
        if i % 2 == 0:
            b_gate, cv = _a_in(x, mixer_norm[i], a_w_in, j, t)
            x = _a_out(b_gate, cv, cache_conv_a[j], taps(a_w_conv[j]), a_w_out, j, x, t)
            p, s = _last_rows(cv, bp, sp, bs, ss, ka - 1)
            state_a_p.append(p)
            state_a_s.append(s)
            x = _ffn(x, ffn_norm[i], ffn_w_gate, ffn_w_up, ffn_w_down, j, t)
        else:
            u = _b_in(x, mixer_norm[i], b_w_pw1, j, t)
            x = _b_out(u, cache_conv_b[j], taps(b_w_dw[j]), b_dw_bias[j], b_ln_g[j], b_ln_b[j], b_w_pw2, j, x, t)
            p, s = _last_rows(u, bp, sp, bs, ss, kb - 1)
            state_b_p.append(p)
            state_b_s.append(s)
            x = _moe(x, ffn_norm[i], moe_w_router[j], moe_w_gate, moe_w_up, moe_w_down, j, t)
    y_prompt, y_sample = _final_norm(x, final_norm, t)
    return (y_prompt.reshape(bp, sp, d), y_sample.reshape(bs, ss, d), jnp.stack(state_a_p), jnp.stack(state_b_p),
            jnp.stack(state_a_s), jnp.stack(state_b_s))


def kernel(x_prompt, x_sample, cache_conv_a, cache_conv_b, mixer_norm, ffn_norm, final_norm, a_w_in, a_w_conv, a_w_out, b_w_pw1, b_w_dw, b_dw_bias, b_ln_g, b_ln_b, b_w_pw2, ffn_w_gate, ffn_w_up, ffn_w_down, moe_w_router, moe_w_gate, moe_w_up, moe_w_down):
    return _forward(x_prompt, x_sample, cache_conv_a, cache_conv_b, mixer_norm, ffn_norm, final_norm,
                    a_w_in, a_w_conv, a_w_out, b_w_pw1, b_w_dw, b_dw_bias, b_ln_g, b_ln_b, b_w_pw2,
                    ffn_w_gate, ffn_w_up, ffn_w_down, moe_w_router, moe_w_gate, moe_w_up, moe_w_down,
                    tm=1024, tn=512, tf=512, te=1024, tc=512)
```
